```python
import jax, jax.numpy as jnp
from jax import lax
import numpy as np

D_MODEL = 1024
BATCH = 8
SEQ = 4096
DEPTH = 1

CONV_WIDTH = D_MODEL // 2
CONV_KERNEL = 31
FOURIER_WIDTH = D_MODEL // 2
FOURIER_GROUPS = 4
FOURIER_GROUP_DIM = FOURIER_WIDTH // FOURIER_GROUPS
N_BRANCHES = 2
IN_COLS = 2 * CONV_WIDTH + FOURIER_WIDTH + N_BRANCHES * D_MODEL
N_EXPERTS = 16
CAPACITY_FACTOR = 2
D_FF_EXPERT = 2 * D_MODEL
N_MOD = 6
RMS_EPS = 1e-6
LN_EPS = 1e-5

kernel_name = "hybrid_conformer_fnet_ec_moe_block"


def rms_norm(x, g):
    xf = x.astype(jnp.float32)
    y = xf * lax.rsqrt(jnp.mean(xf * xf, axis=-1, keepdims=True) + RMS_EPS)
    return (y * g.astype(jnp.float32)).astype(x.dtype)


def layer_norm(x, g, b):
    xf = x.astype(jnp.float32)
    mu = jnp.mean(xf, axis=-1, keepdims=True)
    xc = xf - mu
    var = jnp.mean(xc * xc, axis=-1, keepdims=True)
    y = xc * lax.rsqrt(var + LN_EPS)
    return (y * g.astype(jnp.float32) + b.astype(jnp.float32)).astype(x.dtype)


def modulate(u, shift, scale):
    return u * (1.0 + scale[:, None, :]) + shift[:, None, :]


def conformer_branch(v2, dw_w, dw_b, ln_g, ln_b, w_o, b_o):
    a, g = jnp.split(v2, 2, axis=-1)
    v = a * jax.nn.sigmoid(g)
    pad = CONV_KERNEL // 2
    v = lax.conv_general_dilated(
        v, dw_w[:, None, :].astype(v.dtype), window_strides=(1,),
        padding=[(pad, pad)], dimension_numbers=("NWC", "WIO", "NWC"),
        feature_group_count=CONV_WIDTH) + dw_b
    v = jax.nn.silu(layer_norm(v, ln_g, ln_b))
    return v @ w_o + b_o


def fourier_branch(f, w_f, b_f):
    B, S, _ = f.shape
    fg = f.reshape(B, S, FOURIER_GROUPS, FOURIER_GROUP_DIM).astype(jnp.float32)
    fr = jnp.fft.fft2(fg, axes=(1, 3), norm="ortho").real
    fr = fr.reshape(B, S, FOURIER_WIDTH).astype(f.dtype)
    return fr @ w_f + b_f


def expert_choice_ffn(u, router_w, w_gate, w_up, w_down):
    B, S, D = u.shape
    cap = CAPACITY_FACTOR * S // N_EXPERTS
    logits = jnp.einsum("bsd,de->bse", u.astype(jnp.float32), router_w.astype(jnp.float32))
    aff = jax.nn.softmax(logits, axis=-1)
    aff_t = jnp.transpose(aff, (0, 2, 1))
    w, idx = lax.top_k(aff_t, cap)
    bidx = jnp.arange(B)[:, None, None]
    tok = u[bidx, idx]
    hg = jnp.einsum("becd,edf->becf", tok, w_gate)
    hu = jnp.einsum("becd,edf->becf", tok, w_up)
    out = jnp.einsum("becf,efd->becd", jax.nn.silu(hg) * hu, w_down)
    out = out * w[..., None].astype(out.dtype)
    return jnp.zeros_like(u).at[bidx, idx].add(out)


def setup_inputs(seed: int = 0) -> dict:
    key = jax.random.key(seed)
    ks = jax.random.split(key, 24)
    D, L, E, F = D_MODEL, DEPTH, N_EXPERTS, D_FF_EXPERT
    nrm = lambda k, shape, s: jax.random.normal(k, shape, jnp.float32) * s
    return {
        "x": nrm(ks[0], (BATCH, SEQ, D), 1.0),
        "c": nrm(ks[1], (BATCH, D), 1.0),
        "ada_w": nrm(ks[2], (L, D, N_MOD * D), 0.5 * D ** -0.5),
        "ada_b": nrm(ks[3], (L, N_MOD * D), 0.01),
        "norm1_g": 1.0 + nrm(ks[4], (L, D), 0.01),
        "w_in": nrm(ks[5], (L, D, IN_COLS), D ** -0.5),
        "b_in": nrm(ks[6], (L, IN_COLS), 0.01),
        "conv_dw_w": nrm(ks[7], (L, CONV_KERNEL, CONV_WIDTH), CONV_KERNEL ** -0.5),
        "conv_dw_b": nrm(ks[8], (L, CONV_WIDTH), 0.01),
        "conv_ln_g": 1.0 + nrm(ks[9], (L, CONV_WIDTH), 0.01),
        "conv_ln_b": nrm(ks[10], (L, CONV_WIDTH), 0.01),
        "conv_w_out": nrm(ks[11], (L, CONV_WIDTH, D), CONV_WIDTH ** -0.5),
        "conv_b_out": nrm(ks[12], (L, D), 0.01),
        "fourier_w": nrm(ks[13], (L, FOURIER_WIDTH, D), FOURIER_WIDTH ** -0.5),
        "fourier_b": nrm(ks[14], (L, D), 0.01),
        "w_out": nrm(ks[15], (L, D, D), D ** -0.5),
        "b_out": nrm(ks[16], (L, D), 0.01),
        "norm2_g": 1.0 + nrm(ks[17], (L, D), 0.01),
        "router_w": nrm(ks[18], (L, D, E), D ** -0.5),
        "expert_w_gate": nrm(ks[19], (L, E, D, F), D ** -0.5),
        "expert_w_up": nrm(ks[20], (L, E, D, F), D ** -0.5),
        "expert_w_down": nrm(ks[21], (L, E, F, D), F ** -0.5),
        "final_norm_g": 1.0 + nrm(ks[22], (D,), 0.01),
    }


def reference(x, c, ada_w, ada_b, norm1_g, w_in, b_in, conv_dw_w, conv_dw_b,
              conv_ln_g, conv_ln_b, conv_w_out, conv_b_out, fourier_w, fourier_b,
              w_out, b_out, norm2_g, router_w, expert_w_gate, expert_w_up,
              expert_w_down, final_norm_g):
    B, S, D = x.shape
    h = x
    c_act = jax.nn.silu(c)
    for l in range(DEPTH):
        mod = (c_act @ ada_w[l] + ada_b[l]).reshape(B, N_MOD, D)
        shift1, scale1, gate1 = mod[:, 0], mod[:, 1], mod[:, 2]
        shift2, scale2, gate2 = mod[:, 3], mod[:, 4], mod[:, 5]

        u = modulate(rms_norm(h, norm1_g[l]), shift1, scale1)
        proj = u @ w_in[l] + b_in[l]
        c0 = 2 * CONV_WIDTH
        c1 = c0 + FOURIER_WIDTH
        y_conv = conformer_branch(proj[..., :c0], conv_dw_w[l], conv_dw_b[l],
                                  conv_ln_g[l], conv_ln_b[l], conv_w_out[l], conv_b_out[l])
        y_four = fourier_branch(proj[..., c0:c1], fourier_w[l], fourier_b[l])
        gates = jax.nn.sigmoid(proj[..., c1:].reshape(B, S, N_BRANCHES, D))
        merged = gates[:, :, 0] * y_conv + gates[:, :, 1] * y_four
        h = h + gate1[:, None, :] * (merged @ w_out[l] + b_out[l])

        u2 = modulate(rms_norm(h, norm2_g[l]), shift2, scale2)
        y_ffn = expert_choice_ffn(u2, router_w[l], expert_w_gate[l],
                                  expert_w_up[l], expert_w_down[l])
        h = h + gate2[:, None, :] * y_ffn
    return rms_norm(h, final_norm_g)
```

```python
import functools

import numpy as np
import jax
import jax.numpy as jnp
from jax import lax
from jax.experimental import pallas as pl
from jax.experimental.pallas import tpu as pltpu

F32 = jnp.float32
BF16 = jnp.bfloat16
I32 = jnp.int32

RMS_EPS = 1e-6
LN_EPS = 1e-5
FOURIER_GROUPS = 4
CAPACITY_FACTOR = 2
N_MOD = 6

LANES = 128
SUBLANES = 8
BF16_ROWS = 16
DFT_RADIX = 4
ROUTE_TILE = 256
SLOT_WIN = 64
MIB = 1024 * 1024


def _cparams(sem, vmem_mib):
    return pltpu.CompilerParams(dimension_semantics=sem, vmem_limit_bytes=vmem_mib * MIB)


def _const_spec(shape):
    nd = len(shape)
    return pl.BlockSpec(shape, lambda *_: (0,) * nd)


def _rms_mod(x, g, shift, scale):
    y = x * lax.rsqrt(jnp.mean(x * x, axis=-1, keepdims=True) + RMS_EPS) * g
    return y * (1.0 + scale) + shift


def _bdot(a, b):
    return jnp.dot(a, b, preferred_element_type=F32)


def _ada_kernel(c_ref, w_ref, b_ref, o_ref):
    c = c_ref[...]
    ca = c * jax.nn.sigmoid(c)
    o_ref[...] = _bdot(ca.astype(BF16), w_ref[...].astype(BF16)) + b_ref[...]


def _ada(c, w, b):
    bsz, d = c.shape
    n = w.shape[1]
    tn = min(n, 1536)
    return pl.pallas_call(
        _ada_kernel,
        grid=(n // tn,),
        in_specs=[_const_spec((bsz, d)),
                  pl.BlockSpec((d, tn), lambda j: (0, j)),
                  pl.BlockSpec((1, tn), lambda j: (0, j))],
        out_specs=pl.BlockSpec((bsz, tn), lambda j: (0, j)),
        out_shape=jax.ShapeDtypeStruct((bsz, n), F32),
        compiler_params=_cparams(("arbitrary",), 40),
        name="ada",
    )(c, w, b.reshape(1, n))


def _proj_cf_kernel(x_ref, mod_ref, g_ref, w_ref, b_ref, v_ref, f_ref, *, cw):
    u = _rms_mod(x_ref[0], g_ref[...], mod_ref[0, 0:1, :], mod_ref[0, 1:2, :])
    p = _bdot(u.astype(BF16), w_ref[...]) + b_ref[...]
    a = p[:, :cw]
    g = p[:, cw:2 * cw]
    v_ref[0] = (a * jax.nn.sigmoid(g)).astype(BF16)
    f_ref[0] = p[:, 2 * cw:].astype(BF16)


def _proj_cf(x, mod, g1, w_cf, b_cf, cw, fw, ts):
    bsz, s, d = x.shape
    n = w_cf.shape[1]
    return pl.pallas_call(
        functools.partial(_proj_cf_kernel, cw=cw),
        grid=(bsz, s // ts),
        in_specs=[pl.BlockSpec((1, ts, d), lambda b, j: (b, j, 0)),
                  pl.BlockSpec((1, N_MOD, d), lambda b, j: (b, 0, 0)),
                  _const_spec((1, d)),
                  _const_spec((d, n)),
                  _const_spec((1, n))],
        out_specs=[pl.BlockSpec((1, ts, cw), lambda b, j: (b, j, 0)),
                   pl.BlockSpec((1, ts, fw), lambda b, j: (b, j, 0))],
        out_shape=[jax.ShapeDtypeStruct((bsz, s, cw), BF16),
                   jax.ShapeDtypeStruct((bsz, s, fw), BF16)],
        compiler_params=_cparams(("arbitrary", "arbitrary"), 40),
        name="proj_cf",
    )(x, mod, g1, w_cf, b_cf)


def _dft_constants(s, fw):
    gd = fw // FOURIER_GROUPS
    q = s // DFT_RADIX
    j = np.arange(gd)
    ang = 2.0 * np.pi * np.outer(j, j) / gd
    scale = 1.0 / np.sqrt(float(s) * gd)
    wr = np.zeros((fw, fw), np.float64)
    wi = np.zeros((fw, fw), np.float64)
    for g in range(FOURIER_GROUPS):
        sl = slice(g * gd, (g + 1) * gd)
        wr[sl, sl] = np.cos(ang) * scale
        wi[sl, sl] = -np.sin(ang) * scale
    wg = np.concatenate([wr, wi], axis=1).astype(np.float32)
    b = np.arange(q, dtype=np.int64)
    k = (DFT_RADIX * np.arange(q, dtype=np.int64)[None, :, None]
         + np.arange(DFT_RADIX, dtype=np.int64)[:, None, None])
    m = (k * b[None, None, :]) % s
    ang2 = 2.0 * np.pi * m.astype(np.float64) / s
    tab = np.concatenate([np.cos(ang2), np.sin(ang2)], axis=2).astype(np.float32)
    return wg, tab


def _fourier_kernel(f_ref, wg_ref, tab_ref, o_ref, a_ref, y_ref, *, q, fw, rc, pk):
    wg = wg_ref[...]
    for c in range(q // rc):
        z = []
        for a in range(DFT_RADIX):
            fa = f_ref[0, a * q + c * rc:a * q + (c + 1) * rc, :]
            z.append(_bdot(fa, wg))
        zr = [t[:, :fw] for t in z]
        zi = [t[:, fw:] for t in z]
        er, ei = zr[0] + zr[2], zi[0] + zi[2]
        orr, oi = zr[1] + zr[3], zi[1] + zi[3]
        dr, di = zr[0] - zr[2], zi[0] - zi[2]
        pr, pi_ = zr[1] - zr[3], zi[1] - zi[3]
        ar = [er + orr, dr + pi_, er - orr, dr - pi_]
        ai = [ei + oi, di - pr, ei - oi, di + pr]
        for k1 in range(DFT_RADIX):
            a_ref[k1, c * rc:(c + 1) * rc, :] = ar[k1].astype(BF16)
            a_ref[k1, q + c * rc:q + (c + 1) * rc, :] = ai[k1].astype(BF16)
    for k1 in range(DFT_RADIX):
        y_ref[k1] = _bdot(tab_ref[k1], a_ref[k1]).astype(BF16)
    pt = DFT_RADIX * pk
    r_i = lax.broadcasted_iota(I32, (pt, pt), 0)
    c_i = lax.broadcasted_iota(I32, (pt, pt), 1)
    src = DFT_RADIX * jnp.bitwise_and(c_i, pk - 1) + jnp.right_shift(c_i, pk.bit_length() - 1)
    perm = jnp.where(r_i == src, 1.0, 0.0).astype(BF16)
    for t in range(q // pk):
        yc = jnp.concatenate([y_ref[k1, t * pk:(t + 1) * pk, :] for k1 in range(DFT_RADIX)], axis=0)
        o_ref[0, t * pt:(t + 1) * pt, :] = _bdot(perm, yc).astype(BF16)


def _fourier(f, wg, tab):
    bsz, s, fw = f.shape
    q = s // DFT_RADIX
    rc = min(q, 256)
    pk = min(q, LANES)
    assert pk & (pk - 1) == 0 and q % pk == 0
    return pl.pallas_call(
        functools.partial(_fourier_kernel, q=q, fw=fw, rc=rc, pk=pk),
        grid=(bsz,),
        in_specs=[pl.BlockSpec((1, s, fw), lambda b: (b, 0, 0)),
                  pl.BlockSpec((fw, 2 * fw), lambda b: (0, 0), pipeline_mode=pl.Buffered(1)),
                  pl.BlockSpec((DFT_RADIX, q, 2 * q), lambda b: (0, 0, 0), pipeline_mode=pl.Buffered(1))],
        out_specs=pl.BlockSpec((1, s, fw), lambda b: (b, 0, 0)),
        out_shape=jax.ShapeDtypeStruct((bsz, s, fw), BF16),
        scratch_shapes=[pltpu.VMEM((DFT_RADIX, 2 * q, fw), BF16),
                        pltpu.VMEM((DFT_RADIX, q, fw), BF16)],
        compiler_params=_cparams(("arbitrary",), 56),
        name="fourier",
    )(f, wg, tab)


def _mixer_kernel(x_ref, mod_ref, g1_ref, wg_ref, bg_ref, v_ref, dww_ref, dwb_ref, lng_ref, lnb_ref,
                  wco_ref, bco_ref, fr_ref, wf_ref, bf_ref, wo_ref, bo_ref, g2_ref, rw_ref,
                  h_ref, u2_ref, lg_ref, vwin_ref, *, ts, s, d, kw, halo):
    j = pl.program_id(1)
    nj = pl.num_programs(1)
    s0 = pl.multiple_of(j * ts, ts)
    pad = kw // 2
    x = x_ref[0]
    mod = mod_ref[0]
    u = _rms_mod(x, g1_ref[...], mod[0:1], mod[1:2])
    gates = jax.nn.sigmoid(_bdot(u.astype(BF16), wg_ref[...]) + bg_ref[...])

    vwin_ref[halo:halo + ts, :] = v_ref[0, pl.ds(s0, ts), :].astype(F32)
    lo = pl.multiple_of(jnp.maximum(s0 - halo, 0), halo)
    vwin_ref[0:halo, :] = jnp.where(j > 0, v_ref[0, pl.ds(lo, halo), :].astype(F32), 0.0)
    hi = pl.multiple_of(jnp.minimum(s0 + ts, s - halo), halo)
    vwin_ref[halo + ts:halo + ts + halo, :] = jnp.where(
        j < nj - 1, v_ref[0, pl.ds(hi, halo), :].astype(F32), 0.0)
    first = halo - pad
    acc = None
    for r in range(SUBLANES):
        part = None
        for qq in range((first + kw + SUBLANES - 1) // SUBLANES):
            o = SUBLANES * qq + r
            if first <= o < first + kw:
                term = vwin_ref[SUBLANES * qq:SUBLANES * qq + ts + SUBLANES, :] * dww_ref[o - first:o - first + 1, :]
                part = term if part is None else part + term
        if part is None:
            continue
        shifted = part[:ts] if r == 0 else pltpu.roll(part, ts + SUBLANES - r, axis=0)[:ts]
        acc = shifted + dwb_ref[...] if acc is None else acc + shifted
    mu = jnp.mean(acc, axis=-1, keepdims=True)
    xc = acc - mu
    var = jnp.mean(xc * xc, axis=-1, keepdims=True)
    cv = xc * lax.rsqrt(var + LN_EPS) * lng_ref[...] + lnb_ref[...]
    cv = cv * jax.nn.sigmoid(cv)
    y_conv = _bdot(cv.astype(BF16), wco_ref[...]) + bco_ref[...]
    y_four = _bdot(fr_ref[0], wf_ref[...]) + bf_ref[...]
    merged = gates[:, :d] * y_conv + gates[:, d:] * y_four
    o = _bdot(merged.astype(BF16), wo_ref[...]) + bo_ref[...]
    h = x + mod[2:3] * o
    h_ref[0] = h
    u2 = _rms_mod(h, g2_ref[...], mod[3:4], mod[4:5])
    u2b = u2.astype(BF16)
    u2_ref[0] = u2b
    lg_ref[0] = _bdot(u2b, rw_ref[...])


def _mixer(x, mod, g1, w_g, b_g, v, dww, dwb, lng, lnb, wco, bco, fr, wf, bf, wo, bo, g2, rw, ts):
    bsz, s, d = x.shape
    cw = v.shape[2]
    fw = fr.shape[2]
    kw = dww.shape[0]
    halo = BF16_ROWS
    assert kw // 2 <= halo and s % ts == 0 and ts % halo == 0
    tile = lambda n: pl.BlockSpec((1, ts, n), lambda b, j: (b, j, 0))
    return pl.pallas_call(
        functools.partial(_mixer_kernel, ts=ts, s=s, d=d, kw=kw, halo=halo),
        grid=(bsz, s // ts),
        in_specs=[tile(d),
                  pl.BlockSpec((1, N_MOD, d), lambda b, j: (b, 0, 0)),
                  _const_spec((1, d)),
                  _const_spec((d, 2 * d)), _const_spec((1, 2 * d)),
                  pl.BlockSpec((1, s, cw), lambda b, j: (b, 0, 0)),
                  _const_spec((kw, cw)), _const_spec((1, cw)), _const_spec((1, cw)), _const_spec((1, cw)),
                  _const_spec((cw, d)), _const_spec((1, d)),
                  tile(fw),
                  _const_spec((fw, d)), _const_spec((1, d)),
                  _const_spec((d, d)), _const_spec((1, d)),
                  _const_spec((1, d)),
                  _const_spec((d, LANES))],
        out_specs=[tile(d), tile(d), tile(LANES)],
        out_shape=[jax.ShapeDtypeStruct((bsz, s, d), F32),
                   jax.ShapeDtypeStruct((bsz, s, d), BF16),
                   jax.ShapeDtypeStruct((bsz, s, LANES), F32)],
        scratch_shapes=[pltpu.VMEM((ts + 2 * halo, cw), F32)],
        compiler_params=_cparams(("arbitrary", "arbitrary"), 56),
        name="mixer",
    )(x, mod, g1, w_g, b_g, v, dww, dwb, lng, lnb, wco, bco, fr, wf, bf, wo, bo, g2, rw)


def _route_kernel(lg_ref, grank_ref, aff_ref, offs_ref, *, ne, cap, s, tt):
    lt = lg_ref[0].T[:ne, :]
    m = jnp.max(lt, axis=0, keepdims=True)
    e = jnp.exp(lt - m)
    aff = e / jnp.sum(e, axis=0, keepdims=True)
    aff_ref[0] = aff

    def search(i, thr_bits):
        cand = thr_bits | jnp.left_shift(jnp.int32(1), 30 - i)
        cnt = jnp.sum(jnp.where(aff >= pltpu.bitcast(cand, F32), 1.0, 0.0), axis=1, keepdims=True)
        return jnp.where(cnt >= cap, cand, thr_bits)

    thr = pltpu.bitcast(lax.fori_loop(0, 31, search, jnp.zeros((ne, 1), I32)), F32)
    gt = aff > thr
    tie = aff == thr
    need = cap - jnp.sum(jnp.where(gt, 1.0, 0.0), axis=1, keepdims=True)

    r_i = lax.broadcasted_iota(I32, (tt, tt), 0)
    c_i = lax.broadcasted_iota(I32, (tt, tt), 1)
    upper = jnp.where(r_i < c_i, 1.0, 0.0).astype(BF16)
    offs_ref[0] = jnp.zeros(offs_ref.shape[1:], I32)
    carry_tie = jnp.zeros((ne, 1), F32)
    carry_sel = jnp.zeros((ne, 1), F32)
    for j in range(s // tt):
        sl = slice(j * tt, (j + 1) * tt)
        tie_f = jnp.where(tie[:, sl], 1.0, 0.0)
        tie_rank = _bdot(tie_f.astype(BF16), upper) + carry_tie
        carry_tie = carry_tie + jnp.sum(tie_f, axis=1, keepdims=True)
        sel_f = jnp.where(gt[:, sl], 1.0, jnp.where(tie_rank < need, tie_f, 0.0))
        rank = _bdot(sel_f.astype(BF16), upper) + carry_sel
        offs_ref[0, :, j:j + 1] = carry_sel.astype(I32)
        carry_sel = carry_sel + jnp.sum(sel_f, axis=1, keepdims=True)
        grank_ref[0, :, sl] = jnp.where(sel_f > 0.5, rank.astype(I32), -1)
    offs_ref[0, :, s // tt:s // tt + 1] = carry_sel.astype(I32)


def _route(lg, ne, cap, tt):
    bsz, s, _ = lg.shape
    return pl.pallas_call(
        functools.partial(_route_kernel, ne=ne, cap=cap, s=s, tt=tt),
        grid=(bsz,),
        in_specs=[pl.BlockSpec((1, s, LANES), lambda b: (b, 0, 0))],
        out_specs=[pl.BlockSpec((1, ne, s), lambda b: (b, 0, 0)),
                   pl.BlockSpec((1, ne, s), lambda b: (b, 0, 0)),
                   pl.BlockSpec((1, ne, LANES), lambda b: (b, 0, 0))],
        out_shape=[jax.ShapeDtypeStruct((bsz, ne, s), I32),
                   jax.ShapeDtypeStruct((bsz, ne, s), F32),
                   jax.ShapeDtypeStruct((bsz, ne, LANES), I32)],
        compiler_params=_cparams(("arbitrary",), 40),
        name="route",
    )(lg)


def _round_plan(offs_ref, b, j, ne, nt, cap):
    base = [(b * ne + e) * (nt + 1) + j for e in range(ne)]
    off = [offs_ref[base[e]] for e in range(ne)]
    end = [offs_ref[base[e] + 1] for e in range(ne)]
    w0 = [(off[e] // BF16_ROWS) * BF16_ROWS for e in range(ne)]
    nr = jnp.int32(0)
    for e in range(ne):
        nr_e = jnp.where(end[e] > off[e], (end[e] - w0[e] + SLOT_WIN - 1) // SLOT_WIN, 0)
        nr = jnp.maximum(nr, nr_e)
    return w0, nr


def _one_hot_windows(gr, w0, r, ne, tt, cap):
    srow = lax.broadcasted_iota(I32, (SLOT_WIN, tt), 0)
    starts, masks = [], []
    for e in range(ne):
        lo = w0[e] + SLOT_WIN * r
        start = pl.multiple_of(jnp.minimum(lo, cap - SLOT_WIN), BF16_ROWS)
        g = gr[e:e + 1, :]
        g = jnp.where(g >= lo, g, -1)
        masks.append((g - start) == srow)
        starts.append(start)
    return starts, masks


def _gather_kernel(offs_ref, u2_ref, grank_ref, aff_ref, xg_ref, wrep_ref, *, ne, nt, tt, cap):
    b = pl.program_id(0)
    j = pl.program_id(1)

    @pl.when(j == 0)
    def _():
        xg_ref[...] = jnp.zeros(xg_ref.shape, BF16)
        wrep_ref[...] = jnp.zeros(wrep_ref.shape, F32)

    w0, nr = _round_plan(offs_ref, b, j, ne, nt, cap)
    u2 = u2_ref[0]
    gr = grank_ref[0]
    af = aff_ref[0]

    def round_body(r, carry):
        starts, masks = _one_hot_windows(gr, w0, r, ne, tt, cap)
        pcat = jnp.concatenate([jnp.where(mk, 1.0, 0.0).astype(BF16) for mk in masks], axis=0)
        xw = _bdot(pcat, u2).astype(BF16)
        for e in range(ne):
            rows = pl.ds(starts[e], SLOT_WIN)
            xg_ref[0, e, rows, :] = xg_ref[0, e, rows, :] + xw[e * SLOT_WIN:(e + 1) * SLOT_WIN]
            ws = jnp.sum(jnp.where(masks[e], af[e:e + 1, :], 0.0), axis=1, keepdims=True)
            wrep_ref[0, e, rows, :] = wrep_ref[0, e, rows, :] + jnp.broadcast_to(ws, (SLOT_WIN, LANES))
        return carry

    lax.fori_loop(0, nr, round_body, 0)


def _gather(offs, u2, grank, aff, ne, cap, tt):
    bsz, s, d = u2.shape
    nt = s // tt
    grid_spec = pltpu.PrefetchScalarGridSpec(
        num_scalar_prefetch=1,
        grid=(bsz, nt),
        in_specs=[pl.BlockSpec((1, tt, d), lambda b, j, o: (b, j, 0)),
                  pl.BlockSpec((1, ne, tt), lambda b, j, o: (b, 0, j)),
                  pl.BlockSpec((1, ne, tt), lambda b, j, o: (b, 0, j))],
        out_specs=[pl.BlockSpec((1, ne, cap, d), lambda b, j, o: (b, 0, 0, 0)),
                   pl.BlockSpec((1, ne, cap, LANES), lambda b, j, o: (b, 0, 0, 0))],
    )
    return pl.pallas_call(
        functools.partial(_gather_kernel, ne=ne, nt=nt, tt=tt, cap=cap),
        grid_spec=grid_spec,
        out_shape=[jax.ShapeDtypeStruct((bsz, ne, cap, d), BF16),
                   jax.ShapeDtypeStruct((bsz, ne, cap, LANES), F32)],
        compiler_params=_cparams(("arbitrary", "arbitrary"), 56),
        name="gather",
    )(offs, u2, grank, aff)


def _ffn_kernel(x_ref, wg_ref, wu_ref, wd_ref, wrep_ref, o_ref, acc_ref, *, nb, cap, d):
    f = pl.program_id(2)

    @pl.when(f == 0)
    def _():
        acc_ref[...] = jnp.zeros(acc_ref.shape, F32)

    wg = wg_ref[0].astype(BF16)
    wu = wu_ref[0].astype(BF16)
    wd = wd_ref[0].astype(BF16)
    for i in range(nb):
        x = x_ref[i, 0]
        hg = _bdot(x, wg)
        hu = _bdot(x, wu)
        act = (hg * jax.nn.sigmoid(hg) * hu).astype(BF16)
        acc_ref[i] += _bdot(act, wd)

    @pl.when(f == pl.num_programs(2) - 1)
    def _():
        for i in range(nb):
            w = wrep_ref[i, 0]
            for c in range(d // LANES):
                cols = slice(c * LANES, (c + 1) * LANES)
                o_ref[i, 0, :, cols] = (acc_ref[i, :, cols] * w).astype(BF16)


def _ffn(xg, wrep, w_gate, w_up, w_down, nb, tf):
    bsz, ne, cap, d = xg.shape
    ff = w_gate.shape[2]
    return pl.pallas_call(
        functools.partial(_ffn_kernel, nb=nb, cap=cap, d=d),
        grid=(ne, bsz // nb, ff // tf),
        in_specs=[pl.BlockSpec((nb, 1, cap, d), lambda e, m, f: (m, e, 0, 0)),
                  pl.BlockSpec((1, d, tf), lambda e, m, f: (e, 0, f)),
                  pl.BlockSpec((1, d, tf), lambda e, m, f: (e, 0, f)),
                  pl.BlockSpec((1, tf, d), lambda e, m, f: (e, f, 0)),
                  pl.BlockSpec((nb, 1, cap, LANES), lambda e, m, f: (m, e, 0, 0))],
        out_specs=pl.BlockSpec((nb, 1, cap, d), lambda e, m, f: (m, e, 0, 0)),
        out_shape=jax.ShapeDtypeStruct((bsz, ne, cap, d), BF16),
        scratch_shapes=[pltpu.VMEM((nb, cap, d), F32)],
        compiler_params=_cparams(("arbitrary", "arbitrary", "arbitrary"), 56),
        name="ffn",
    )(xg, w_gate, w_up, w_down, wrep)


def _combine_kernel(offs_ref, grank_ref, eo_ref, h_ref, mod_ref, gf_ref, o_ref, y_ref, *, ne, nt, tt, cap, final):
    b = pl.program_id(0)
    j = pl.program_id(1)
    w0, nr = _round_plan(offs_ref, b, j, ne, nt, cap)
    gr = grank_ref[0]
    y_ref[...] = jnp.zeros(y_ref.shape, F32)

    def round_body(r, carry):
        starts, masks = _one_hot_windows(gr, w0, r, ne, tt, cap)
        pcat = jnp.concatenate([jnp.where(mk, 1.0, 0.0).astype(BF16) for mk in masks], axis=0)
        ocat = jnp.concatenate([eo_ref[0, e, pl.ds(starts[e], SLOT_WIN), :] for e in range(ne)], axis=0)
        y_ref[...] = y_ref[...] + lax.dot_general(pcat, ocat, (((0,), (0,)), ((), ())),
                                                  preferred_element_type=F32)
        return carry

    lax.fori_loop(0, nr, round_body, 0)
    hout = h_ref[0] + mod_ref[0, 5:6, :] * y_ref[...]
    if final:
        hout = hout * lax.rsqrt(jnp.mean(hout * hout, axis=-1, keepdims=True) + RMS_EPS) * gf_ref[...]
    o_ref[0] = hout


def _combine(offs, grank, eo, h, mod, gf, tt, final):
    bsz, s, d = h.shape
    ne, cap = eo.shape[1], eo.shape[2]
    nt = s // tt
    grid_spec = pltpu.PrefetchScalarGridSpec(
        num_scalar_prefetch=1,
        grid=(bsz, nt),
        in_specs=[pl.BlockSpec((1, ne, tt), lambda b, j, o: (b, 0, j)),
                  pl.BlockSpec((1, ne, cap, d), lambda b, j, o: (b, 0, 0, 0)),
                  pl.BlockSpec((1, tt, d), lambda b, j, o: (b, j, 0)),
                  pl.BlockSpec((1, N_MOD, d), lambda b, j, o: (b, 0, 0)),
                  pl.BlockSpec((1, d), lambda b, j, o: (0, 0))],
        out_specs=pl.BlockSpec((1, tt, d), lambda b, j, o: (b, j, 0)),
        scratch_shapes=[pltpu.VMEM((tt, d), F32)],
    )
    return pl.pallas_call(
        functools.partial(_combine_kernel, ne=ne, nt=nt, tt=tt, cap=cap, final=final),
        grid_spec=grid_spec,
        out_shape=jax.ShapeDtypeStruct((bsz, s, d), F32),
        compiler_params=_cparams(("arbitrary", "arbitrary"), 56),
        name="combine",
    )(offs, grank, eo, h, mod, gf)


def kernel(x, c, ada_w, ada_b, norm1_g, w_in, b_in, conv_dw_w, conv_dw_b, conv_ln_g, conv_ln_b,
           conv_w_out, conv_b_out, fourier_w, fourier_b, w_out, b_out, norm2_g, router_w,
           expert_w_gate, expert_w_up, expert_w_down, final_norm_g):
    bsz, s, d = x.shape
    depth = ada_w.shape[0]
    cw = conv_dw_w.shape[2]
    fw = fourier_w.shape[1]
    ne = router_w.shape[2]
    cap = CAPACITY_FACTOR * s // ne
    c0 = 2 * cw
    c1 = c0 + fw
    tt = min(ROUTE_TILE, s)
    ts = min(512, s)
    assert s % DFT_RADIX == 0 and s % tt == 0 and cap % BF16_ROWS == 0 and cap >= SLOT_WIN and ne <= LANES
    wg_np, tab_np = _dft_constants(s, fw)
    dft_wg = jnp.asarray(wg_np).astype(BF16)
    dft_tab = jnp.asarray(tab_np).astype(BF16)
    row = lambda a: a.reshape(1, -1)

    h = x
    for l in range(depth):
        mod = _ada(c, ada_w[l], ada_b[l]).reshape(bsz, N_MOD, d)
        w_in_b = w_in[l].astype(BF16)
        v, f = _proj_cf(h, mod, row(norm1_g[l]), w_in_b[:, :c1], row(b_in[l][:c1]), cw, fw, ts)
        fr = _fourier(f, dft_wg, dft_tab)
        rw = jnp.zeros((d, LANES), BF16).at[:, :ne].set(router_w[l].astype(BF16))
        h, u2, lg = _mixer(h, mod, row(norm1_g[l]), w_in_b[:, c1:], row(b_in[l][c1:]), v,
                           conv_dw_w[l], row(conv_dw_b[l]), row(conv_ln_g[l]), row(conv_ln_b[l]),
                           conv_w_out[l].astype(BF16), row(conv_b_out[l]), fr,
                           fourier_w[l].astype(BF16), row(fourier_b[l]),
                           w_out[l].astype(BF16), row(b_out[l]), row(norm2_g[l]), rw, ts)
        grank, aff, offs = _route(lg, ne, cap, tt)
        offs_flat = offs[:, :, :s // tt + 1].reshape(-1)
        xg, wrep = _gather(offs_flat, u2, grank, aff, ne, cap, tt)
        nb = 4 if bsz % 4 == 0 else 1
        eo = _ffn(xg, wrep, expert_w_gate[l], expert_w_up[l], expert_w_down[l], nb, min(512, expert_w_gate.shape[3]))
        h = _combine(offs_flat, grank, eo, h, mod, row(final_norm_g), tt, final=(l == depth - 1))
    return h
```

```python
import functools

import numpy as np
import jax
import jax.numpy as jnp
from jax import lax
from jax.experimental import pallas as pl
from jax.experimental.pallas import tpu as pltpu

F32 = jnp.float32
BF16 = jnp.bfloat16
I32 = jnp.int32

RMS_EPS = 1e-6
LN_EPS = 1e-5
FOURIER_GROUPS = 4
CAPACITY_FACTOR = 2
N_MOD = 6

LANES = 128
SUBLANES = 8
BF16_ROWS = 16
DFT_RADIX = 4
ROUTE_TILE = 256
SLOT_WIN = 64
MIB = 1024 * 1024


def _cparams(sem, vmem_mib):
    return pltpu.CompilerParams(dimension_semantics=sem, vmem_limit_bytes=vmem_mib * MIB)


def _const_spec(shape):
    nd = len(shape)
    return pl.BlockSpec(shape, lambda *_: (0,) * nd)


def _rms_mod(x, g, shift, scale):
    return x * lax.rsqrt(jnp.mean(x * x, axis=-1, keepdims=True) + RMS_EPS) * (g * (1.0 + scale)) + shift


def _bdot(a, b):
    return jnp.dot(a, b, preferred_element_type=F32)


def _ada_kernel(c_ref, w_ref, b_ref, o_ref):
    c = c_ref[...]
    ca = c * jax.nn.sigmoid(c)
    o_ref[...] = _bdot(ca.astype(BF16), w_ref[...].astype(BF16)) + b_ref[...]


def _ada(c, w, b):
    bsz, d = c.shape
    n = w.shape[1]
    tn = min(n, 1536)
    return pl.pallas_call(
        _ada_kernel,
        grid=(n // tn,),
        in_specs=[_const_spec((bsz, d)),
                  pl.BlockSpec((d, tn), lambda j: (0, j)),
                  pl.BlockSpec((1, tn), lambda j: (0, j))],
        out_specs=pl.BlockSpec((bsz, tn), lambda j: (0, j)),
        out_shape=jax.ShapeDtypeStruct((bsz, n), F32),
        compiler_params=_cparams(("arbitrary",), 40),
        name="ada",
    )(c, w, b.reshape(1, n))


def _proj_cf_kernel(x_ref, mod_ref, g_ref, w_ref, b_ref, v_ref, f_ref, *, cw):
    u = _rms_mod(x_ref[0], g_ref[...], mod_ref[0, 0:1, :], mod_ref[0, 1:2, :])
    p = _bdot(u.astype(BF16), w_ref[...]) + b_ref[...]
    a = p[:, :cw]
    g = p[:, cw:2 * cw]
    v_ref[0] = (a * jax.nn.sigmoid(g)).astype(BF16)
    f_ref[0] = p[:, 2 * cw:].astype(BF16)


def _proj_cf(x, mod, g1, w_cf, b_cf, cw, fw, ts):
    bsz, s, d = x.shape
    n = w_cf.shape[1]
    tile = lambda m: pl.BlockSpec((1, ts, m), lambda b, j: (b, j, 0))
    return pl.pallas_call(
        functools.partial(_proj_cf_kernel, cw=cw),
        grid=(bsz, s // ts),
        in_specs=[tile(d),
                  pl.BlockSpec((1, N_MOD, d), lambda b, j: (b, 0, 0)),
                  _const_spec((1, d)),
                  _const_spec((d, n)),
                  _const_spec((1, n))],
        out_specs=[tile(cw), tile(fw)],
        out_shape=[jax.ShapeDtypeStruct((bsz, s, cw), BF16),
                   jax.ShapeDtypeStruct((bsz, s, fw), BF16)],
        compiler_params=_cparams(("arbitrary", "arbitrary"), 40),
        name="proj_cf",
    )(x, mod, g1, w_cf, b_cf)


def _group_block(fw):
    gd = fw // FOURIER_GROUPS
    return gd * max(1, min(FOURIER_GROUPS, LANES // gd))


def _dft_constants(s, fw):
    gd = fw // FOURIER_GROUPS
    q = s // DFT_RADIX
    j = np.arange(gd)
    ang = 2.0 * np.pi * np.outer(j, j) / gd
    scale = 1.0 / np.sqrt(float(s) * gd)
    wb = _group_block(fw)
    wr = np.zeros((wb, wb), np.float64)
    wi = np.zeros((wb, wb), np.float64)
    for g in range(wb // gd):
        sl = slice(g * gd, (g + 1) * gd)
        wr[sl, sl] = np.cos(ang) * scale
        wi[sl, sl] = -np.sin(ang) * scale
    wg = np.concatenate([wr, wi], axis=1).astype(np.float32)
    b = np.arange(q, dtype=np.int64)
    k = (DFT_RADIX * np.arange(q, dtype=np.int64)[None, :, None]
         + np.arange(DFT_RADIX, dtype=np.int64)[:, None, None])
    m = (k * b[None, None, :]) % s
    ang2 = 2.0 * np.pi * m.astype(np.float64) / s
    tab = np.concatenate([np.cos(ang2), np.sin(ang2)], axis=2).astype(np.float32)
    return wg, tab


def _fourier_kernel(f_ref, wg_ref, tab_ref, o_ref, a_ref, y_ref, *, q, fw, rc, pk):
    wg = wg_ref[...]
    wb = wg.shape[0]
    for c in range(q // rc):
        zr, zi = [], []
        for a in range(DFT_RADIX):
            fa = f_ref[0, a * q + c * rc:a * q + (c + 1) * rc, :]
            z = [_bdot(fa[:, g * wb:(g + 1) * wb], wg) for g in range(fw // wb)]
            zr.append(jnp.concatenate([t[:, :wb] for t in z], axis=1))
            zi.append(jnp.concatenate([t[:, wb:] for t in z], axis=1))
        er, ei = zr[0] + zr[2], zi[0] + zi[2]
        orr, oi = zr[1] + zr[3], zi[1] + zi[3]
        dr, di = zr[0] - zr[2], zi[0] - zi[2]
        pr, pi_ = zr[1] - zr[3], zi[1] - zi[3]
        ar = [er + orr, dr + pi_, er - orr, dr - pi_]
        ai = [ei + oi, di - pr, ei - oi, di + pr]
        for k1 in range(DFT_RADIX):
            a_ref[k1, c * rc:(c + 1) * rc, :] = ar[k1].astype(BF16)
            a_ref[k1, q + c * rc:q + (c + 1) * rc, :] = ai[k1].astype(BF16)
    for k1 in range(DFT_RADIX):
        y_ref[k1] = _bdot(tab_ref[k1], a_ref[k1]).astype(BF16)
    pt = DFT_RADIX * pk
    r_i = lax.broadcasted_iota(I32, (pt, pt), 0)
    c_i = lax.broadcasted_iota(I32, (pt, pt), 1)
    src = DFT_RADIX * jnp.bitwise_and(c_i, pk - 1) + jnp.right_shift(c_i, pk.bit_length() - 1)
    perm = jnp.where(r_i == src, 1.0, 0.0).astype(BF16)
    for t in range(q // pk):
        yc = jnp.concatenate([y_ref[k1, t * pk:(t + 1) * pk, :] for k1 in range(DFT_RADIX)], axis=0)
        o_ref[0, t * pt:(t + 1) * pt, :] = _bdot(perm, yc).astype(BF16)


def _fourier(f, wg, tab):
    bsz, s, fw = f.shape
    q = s // DFT_RADIX
    rc = min(q, 256)
    pk = min(q, LANES)
    assert pk & (pk - 1) == 0 and q % pk == 0
    return pl.pallas_call(
        functools.partial(_fourier_kernel, q=q, fw=fw, rc=rc, pk=pk),
        grid=(bsz,),
        in_specs=[pl.BlockSpec((1, s, fw), lambda b: (b, 0, 0)),
                  pl.BlockSpec(wg.shape, lambda b: (0, 0), pipeline_mode=pl.Buffered(1)),
                  pl.BlockSpec((DFT_RADIX, q, 2 * q), lambda b: (0, 0, 0), pipeline_mode=pl.Buffered(1))],
        out_specs=pl.BlockSpec((1, s, fw), lambda b: (b, 0, 0)),
        out_shape=jax.ShapeDtypeStruct((bsz, s, fw), BF16),
        scratch_shapes=[pltpu.VMEM((DFT_RADIX, 2 * q, fw), BF16),
                        pltpu.VMEM((DFT_RADIX, q, fw), BF16)],
        compiler_params=_cparams(("arbitrary",), 56),
        name="fourier",
    )(f, wg, tab)


def _mixer_kernel(x_ref, mod_ref, g1_ref, wg_ref, bg_ref, v_ref, dww_ref, dwb_ref, lng_ref, lnb_ref,
                  wco_ref, bco_ref, fr_ref, wf_ref, bf_ref, wo_ref, bo_ref, g2_ref, rw_ref,
                  h_ref, u2_ref, lg_ref, vwin_ref, *, ts, s, d, kw, halo):
    j = pl.program_id(1)
    nj = pl.num_programs(1)
    s0 = pl.multiple_of(j * ts, ts)
    pad = kw // 2
    x = x_ref[0]
    mod = mod_ref[0]
    u = _rms_mod(x, g1_ref[...], mod[0:1], mod[1:2])
    gates = jax.nn.sigmoid(_bdot(u.astype(BF16), wg_ref[...]) + bg_ref[...])

    vwin_ref[halo:halo + ts, :] = v_ref[0, pl.ds(s0, ts), :].astype(F32)
    lo = pl.multiple_of(jnp.maximum(s0 - halo, 0), halo)
    vwin_ref[0:halo, :] = jnp.where(j > 0, v_ref[0, pl.ds(lo, halo), :].astype(F32), 0.0)
    hi = pl.multiple_of(jnp.minimum(s0 + ts, s - halo), halo)
    vwin_ref[halo + ts:halo + ts + halo, :] = jnp.where(
        j < nj - 1, v_ref[0, pl.ds(hi, halo), :].astype(F32), 0.0)
    first = halo - pad
    acc = None
    for r in range(SUBLANES):
        part = None
        for qq in range((first + kw + SUBLANES - 1) // SUBLANES):
            o = SUBLANES * qq + r
            if first <= o < first + kw:
                term = vwin_ref[SUBLANES * qq:SUBLANES * qq + ts + SUBLANES, :] * dww_ref[o - first:o - first + 1, :]
                part = term if part is None else part + term
        if part is None:
            continue
        shifted = part[:ts] if r == 0 else pltpu.roll(part, ts + SUBLANES - r, axis=0)[:ts]
        acc = shifted + dwb_ref[...] if acc is None else acc + shifted
    mu = jnp.mean(acc, axis=-1, keepdims=True)
    xc = acc - mu
    var = jnp.mean(xc * xc, axis=-1, keepdims=True)
    cv = xc * lax.rsqrt(var + LN_EPS) * lng_ref[...] + lnb_ref[...]
    cv = cv * jax.nn.sigmoid(cv)
    y_conv = _bdot(cv.astype(BF16), wco_ref[...]) + bco_ref[...]
    y_four = _bdot(fr_ref[0], wf_ref[...]) + bf_ref[...]
    merged = gates[:, :d] * y_conv + gates[:, d:] * y_four
    o = _bdot(merged.astype(BF16), wo_ref[...]) + bo_ref[...]
    h = x + mod[2:3] * o
    h_ref[0] = h
    u2 = _rms_mod(h, g2_ref[...], mod[3:4], mod[4:5])
    u2b = u2.astype(BF16)
    u2_ref[0] = u2b
    lg_ref[0] = _bdot(u2b, rw_ref[...])


def _mixer(x, mod, g1, w_g, b_g, v, dww, dwb, lng, lnb, wco, bco, fr, wf, bf, wo, bo, g2, rw, ts):
    bsz, s, d = x.shape
    cw = v.shape[2]
    fw = fr.shape[2]
    kw = dww.shape[0]
    halo = BF16_ROWS
    assert kw // 2 <= halo and s % ts == 0 and ts % halo == 0
    tile = lambda n: pl.BlockSpec((1, ts, n), lambda b, j: (b, j, 0))
    return pl.pallas_call(
        functools.partial(_mixer_kernel, ts=ts, s=s, d=d, kw=kw, halo=halo),
        grid=(bsz, s // ts),
        in_specs=[tile(d),
                  pl.BlockSpec((1, N_MOD, d), lambda b, j: (b, 0, 0)),
                  _const_spec((1, d)),
                  _const_spec((d, 2 * d)), _const_spec((1, 2 * d)),
                  pl.BlockSpec((1, s, cw), lambda b, j: (b, 0, 0)),
                  _const_spec((kw, cw)), _const_spec((1, cw)), _const_spec((1, cw)), _const_spec((1, cw)),
                  _const_spec((cw, d)), _const_spec((1, d)),
                  tile(fw),
                  _const_spec((fw, d)), _const_spec((1, d)),
                  _const_spec((d, d)), _const_spec((1, d)),
                  _const_spec((1, d)),
                  _const_spec((d, LANES))],
        out_specs=[tile(d), tile(d), tile(LANES)],
        out_shape=[jax.ShapeDtypeStruct((bsz, s, d), F32),
                   jax.ShapeDtypeStruct((bsz, s, d), BF16),
                   jax.ShapeDtypeStruct((bsz, s, LANES), F32)],
        scratch_shapes=[pltpu.VMEM((ts + 2 * halo, cw), F32)],
        compiler_params=_cparams(("arbitrary", "arbitrary"), 56),
        name="mixer",
    )(x, mod, g1, w_g, b_g, v, dww, dwb, lng, lnb, wco, bco, fr, wf, bf, wo, bo, g2, rw)


def _route_kernel(lg_ref, grank_ref, aff_ref, offs_ref, *, ne, cap, s, tt):
    lt = lg_ref[0].T[:ne, :]
    m = jnp.max(lt, axis=0, keepdims=True)
    e = jnp.exp(lt - m)
    aff = e / jnp.sum(e, axis=0, keepdims=True)
    aff_ref[0] = aff

    def search(i, thr_bits):
        cand = thr_bits | jnp.left_shift(jnp.int32(1), 30 - i)
        cnt = jnp.sum(jnp.where(aff >= pltpu.bitcast(cand, F32), 1.0, 0.0), axis=1, keepdims=True)
        return jnp.where(cnt >= cap, cand, thr_bits)

    thr = pltpu.bitcast(lax.fori_loop(0, 31, search, jnp.zeros((ne, 1), I32)), F32)
    gt = aff > thr
    tie = aff == thr
    need = cap - jnp.sum(jnp.where(gt, 1.0, 0.0), axis=1, keepdims=True)

    r_i = lax.broadcasted_iota(I32, (tt, tt), 0)
    c_i = lax.broadcasted_iota(I32, (tt, tt), 1)
    upper = jnp.where(r_i < c_i, 1.0, 0.0).astype(BF16)
    offs_ref[0] = jnp.zeros(offs_ref.shape[1:], I32)
    carry_tie = jnp.zeros((ne, 1), F32)
    carry_sel = jnp.zeros((ne, 1), F32)
    for j in range(s // tt):
        sl = slice(j * tt, (j + 1) * tt)
        tie_f = jnp.where(tie[:, sl], 1.0, 0.0)
        tie_rank = _bdot(tie_f.astype(BF16), upper) + carry_tie
        carry_tie = carry_tie + jnp.sum(tie_f, axis=1, keepdims=True)
        sel_f = jnp.where(gt[:, sl], 1.0, jnp.where(tie_rank < need, tie_f, 0.0))
        rank = _bdot(sel_f.astype(BF16), upper) + carry_sel
        offs_ref[0, :, j:j + 1] = carry_sel.astype(I32)
        carry_sel = carry_sel + jnp.sum(sel_f, axis=1, keepdims=True)
        grank_ref[0, :, sl] = jnp.where(sel_f > 0.5, rank.astype(I32), -1)
    offs_ref[0, :, s // tt:s // tt + 1] = carry_sel.astype(I32)


def _route(lg, ne, cap, tt):
    bsz, s, _ = lg.shape
    return pl.pallas_call(
        functools.partial(_route_kernel, ne=ne, cap=cap, s=s, tt=tt),
        grid=(bsz,),
        in_specs=[pl.BlockSpec((1, s, LANES), lambda b: (b, 0, 0))],
        out_specs=[pl.BlockSpec((1, ne, s), lambda b: (b, 0, 0)),
                   pl.BlockSpec((1, ne, s), lambda b: (b, 0, 0)),
                   pl.BlockSpec((1, ne, LANES), lambda b: (b, 0, 0))],
        out_shape=[jax.ShapeDtypeStruct((bsz, ne, s), I32),
                   jax.ShapeDtypeStruct((bsz, ne, s), F32),
                   jax.ShapeDtypeStruct((bsz, ne, LANES), I32)],
        compiler_params=_cparams(("arbitrary",), 40),
        name="route",
    )(lg)


def _round_plan(offs_ref, b, j, ne, nt, cap):
    base = [(b * ne + e) * (nt + 1) + j for e in range(ne)]
    off = [offs_ref[base[e]] for e in range(ne)]
    end = [offs_ref[base[e] + 1] for e in range(ne)]
    w0 = [(off[e] // BF16_ROWS) * BF16_ROWS for e in range(ne)]
    nr = jnp.int32(0)
    for e in range(ne):
        nr_e = jnp.where(end[e] > off[e], (end[e] - w0[e] + SLOT_WIN - 1) // SLOT_WIN, 0)
        nr = jnp.maximum(nr, nr_e)
    return w0, nr


def _one_hot_windows(gr, w0, r, ne, tt, cap):
    srow = lax.broadcasted_iota(I32, (SLOT_WIN, tt), 0)
    starts, masks = [], []
    for e in range(ne):
        lo = w0[e] + SLOT_WIN * r
        start = pl.multiple_of(jnp.minimum(lo, cap - SLOT_WIN), BF16_ROWS)
        g = gr[e:e + 1, :]
        g = jnp.where(g >= lo, g, -1)
        masks.append((g - start) == srow)
        starts.append(start)
    return starts, masks


def _gather_kernel(offs_ref, u2_ref, grank_ref, aff_ref, xg_ref, wrep_ref, *, ne, nt, tt, cap, nsub):
    b = pl.program_id(0)
    j = pl.program_id(1)

    @pl.when(j == 0)
    def _():
        xg_ref[...] = jnp.zeros(xg_ref.shape, BF16)
        wrep_ref[...] = jnp.zeros(wrep_ref.shape, F32)

    for sub in range(nsub):
        w0, nr = _round_plan(offs_ref, b, j * nsub + sub, ne, nt, cap)
        u2 = u2_ref[0, sub * tt:(sub + 1) * tt, :]
        gr = grank_ref[0, :, sub * tt:(sub + 1) * tt]
        af = aff_ref[0, :, sub * tt:(sub + 1) * tt]

        def round_body(r, carry, w0=w0, u2=u2, gr=gr, af=af):
            starts, masks = _one_hot_windows(gr, w0, r, ne, tt, cap)
            pcat = jnp.concatenate([jnp.where(mk, 1.0, 0.0).astype(BF16) for mk in masks], axis=0)
            xw = _bdot(pcat, u2).astype(BF16)
            for e in range(ne):
                rows = pl.ds(starts[e], SLOT_WIN)
                xg_ref[0, e, rows, :] = xg_ref[0, e, rows, :] + xw[e * SLOT_WIN:(e + 1) * SLOT_WIN]
                ws = jnp.sum(jnp.where(masks[e], af[e:e + 1, :], 0.0), axis=1, keepdims=True)
                wrep_ref[0, e, rows, :] = wrep_ref[0, e, rows, :] + jnp.broadcast_to(ws, (SLOT_WIN, LANES))
            return carry

        lax.fori_loop(0, nr, round_body, 0)


def _gather(offs, u2, grank, aff, ne, cap, tt, nsub):
    bsz, s, d = u2.shape
    nt = s // tt
    tg = nsub * tt
    grid_spec = pltpu.PrefetchScalarGridSpec(
        num_scalar_prefetch=1,
        grid=(bsz, nt // nsub),
        in_specs=[pl.BlockSpec((1, tg, d), lambda b, j, o: (b, j, 0)),
                  pl.BlockSpec((1, ne, tg), lambda b, j, o: (b, 0, j)),
                  pl.BlockSpec((1, ne, tg), lambda b, j, o: (b, 0, j))],
        out_specs=[pl.BlockSpec((1, ne, cap, d), lambda b, j, o: (b, 0, 0, 0)),
                   pl.BlockSpec((1, ne, cap, LANES), lambda b, j, o: (b, 0, 0, 0))],
    )
    return pl.pallas_call(
        functools.partial(_gather_kernel, ne=ne, nt=nt, tt=tt, cap=cap, nsub=nsub),
        grid_spec=grid_spec,
        out_shape=[jax.ShapeDtypeStruct((bsz, ne, cap, d), BF16),
                   jax.ShapeDtypeStruct((bsz, ne, cap, LANES), F32)],
        compiler_params=_cparams(("arbitrary", "arbitrary"), 56),
        name="gather",
    )(offs, u2, grank, aff)


def _ffn_kernel(x_ref, wg_ref, wu_ref, wd_ref, wrep_ref, o_ref, acc_ref, *, nb, cap, d):
    f = pl.program_id(2)
    last = pl.num_programs(2) - 1
    wg = wg_ref[0].astype(BF16)
    wu = wu_ref[0].astype(BF16)
    wd = wd_ref[0].astype(BF16)

    def partial_out(i):
        x = x_ref[i, 0]
        hg = _bdot(x, wg)
        hu = _bdot(x, wu)
        act = (hg * jax.nn.sigmoid(hg) * hu).astype(BF16)
        return _bdot(act, wd)

    @pl.when(f == 0)
    def _():
        for i in range(nb):
            acc_ref[i] = partial_out(i)

    @pl.when(jnp.logical_and(f > 0, f < last))
    def _():
        for i in range(nb):
            acc_ref[i] += partial_out(i)

    @pl.when(f == last)
    def _():
        for i in range(nb):
            tot = acc_ref[i] + partial_out(i)
            w = wrep_ref[i, 0]
            for c in range(d // LANES):
                cols = slice(c * LANES, (c + 1) * LANES)
                o_ref[i, 0, :, cols] = (tot[:, cols] * w).astype(BF16)


def _ffn(xg, wrep, w_gate, w_up, w_down, nb, tf):
    bsz, ne, cap, d = xg.shape
    ff = w_gate.shape[2]
    assert ff % tf == 0 and ff // tf >= 2
    return pl.pallas_call(
        functools.partial(_ffn_kernel, nb=nb, cap=cap, d=d),
        grid=(ne, bsz // nb, ff // tf),
        in_specs=[pl.BlockSpec((nb, 1, cap, d), lambda e, m, f: (m, e, 0, 0)),
                  pl.BlockSpec((1, d, tf), lambda e, m, f: (e, 0, f)),
                  pl.BlockSpec((1, d, tf), lambda e, m, f: (e, 0, f)),
                  pl.BlockSpec((1, tf, d), lambda e, m, f: (e, f, 0)),
                  pl.BlockSpec((nb, 1, cap, LANES), lambda e, m, f: (m, e, 0, 0))],
        out_specs=pl.BlockSpec((nb, 1, cap, d), lambda e, m, f: (m, e, 0, 0)),
        out_shape=jax.ShapeDtypeStruct((bsz, ne, cap, d), BF16),
        scratch_shapes=[pltpu.VMEM((nb, cap, d), F32)],
        compiler_params=_cparams(("arbitrary", "arbitrary", "arbitrary"), 56),
        name="ffn",
    )(xg, w_gate, w_up, w_down, wrep)


def _combine_kernel(offs_ref, grank_ref, eo_ref, h_ref, mod_ref, gf_ref, o_ref, y_ref, *,
                    ne, nt, tt, cap, nsub, final):
    b = pl.program_id(0)
    j = pl.program_id(1)
    y_ref[...] = jnp.zeros(y_ref.shape, F32)
    for sub in range(nsub):
        w0, nr = _round_plan(offs_ref, b, j * nsub + sub, ne, nt, cap)
        gr = grank_ref[0, :, sub * tt:(sub + 1) * tt]
        rows = slice(sub * tt, (sub + 1) * tt)

        def round_body(r, carry, w0=w0, gr=gr, rows=rows):
            starts, masks = _one_hot_windows(gr, w0, r, ne, tt, cap)
            pcat = jnp.concatenate([jnp.where(mk, 1.0, 0.0).astype(BF16) for mk in masks], axis=0)
            ocat = jnp.concatenate([eo_ref[0, e, pl.ds(starts[e], SLOT_WIN), :] for e in range(ne)], axis=0)
            y_ref[rows, :] = y_ref[rows, :] + lax.dot_general(pcat, ocat, (((0,), (0,)), ((), ())),
                                                              preferred_element_type=F32)
            return carry

        lax.fori_loop(0, nr, round_body, 0)
    hout = h_ref[0] + mod_ref[0, 5:6, :] * y_ref[...]
    if final:
        hout = hout * lax.rsqrt(jnp.mean(hout * hout, axis=-1, keepdims=True) + RMS_EPS) * gf_ref[...]
    o_ref[0] = hout


def _combine(offs, grank, eo, h, mod, gf, tt, nsub, final):
    bsz, s, d = h.shape
    ne, cap = eo.shape[1], eo.shape[2]
    nt = s // tt
    tg = nsub * tt
    grid_spec = pltpu.PrefetchScalarGridSpec(
        num_scalar_prefetch=1,
        grid=(bsz, nt // nsub),
        in_specs=[pl.BlockSpec((1, ne, tg), lambda b, j, o: (b, 0, j)),
                  pl.BlockSpec((1, ne, cap, d), lambda b, j, o: (b, 0, 0, 0)),
                  pl.BlockSpec((1, tg, d), lambda b, j, o: (b, j, 0)),
                  pl.BlockSpec((1, N_MOD, d), lambda b, j, o: (b, 0, 0)),
                  pl.BlockSpec((1, d), lambda b, j, o: (0, 0))],
        out_specs=pl.BlockSpec((1, tg, d), lambda b, j, o: (b, j, 0)),
        scratch_shapes=[pltpu.VMEM((tg, d), F32)],
    )
    return pl.pallas_call(
        functools.partial(_combine_kernel, ne=ne, nt=nt, tt=tt, cap=cap, nsub=nsub, final=final),
        grid_spec=grid_spec,
        out_shape=jax.ShapeDtypeStruct((bsz, s, d), F32),
        compiler_params=_cparams(("arbitrary", "arbitrary"), 56),
        name="combine",
    )(offs, grank, eo, h, mod, gf)


def kernel(x, c, ada_w, ada_b, norm1_g, w_in, b_in, conv_dw_w, conv_dw_b, conv_ln_g, conv_ln_b,
           conv_w_out, conv_b_out, fourier_w, fourier_b, w_out, b_out, norm2_g, router_w,
           expert_w_gate, expert_w_up, expert_w_down, final_norm_g):
    bsz, s, d = x.shape
    depth = ada_w.shape[0]
    cw = conv_dw_w.shape[2]
    fw = fourier_w.shape[1]
    ne = router_w.shape[2]
    cap = CAPACITY_FACTOR * s // ne
    tt = min(ROUTE_TILE, s)
    ts = min(512, s)
    assert s % DFT_RADIX == 0 and s % tt == 0 and cap % BF16_ROWS == 0 and cap >= SLOT_WIN and ne <= LANES
    wg_np, tab_np = _dft_constants(s, fw)
    dft_wg = jnp.asarray(wg_np).astype(BF16)
    dft_tab = jnp.asarray(tab_np).astype(BF16)
    row = lambda a: a.reshape(1, -1)

    h = x
    for l in range(depth):
        mod = _ada(c, ada_w[l], ada_b[l]).reshape(bsz, N_MOD, d)
        c1 = 2 * cw + fw
        w_in_b = w_in[l].astype(BF16)
        v, f = _proj_cf(h, mod, row(norm1_g[l]), w_in_b[:, :c1], row(b_in[l][:c1]), cw, fw, ts)
        fr = _fourier(f, dft_wg, dft_tab)
        rw = jnp.zeros((d, LANES), BF16).at[:, :ne].set(router_w[l].astype(BF16))
        h, u2, lg = _mixer(h, mod, row(norm1_g[l]), w_in_b[:, c1:], row(b_in[l][c1:]), v,
                           conv_dw_w[l], row(conv_dw_b[l]), row(conv_ln_g[l]), row(conv_ln_b[l]),
                           conv_w_out[l].astype(BF16), row(conv_b_out[l]), fr,
                           fourier_w[l].astype(BF16), row(fourier_b[l]),
                           w_out[l].astype(BF16), row(b_out[l]), row(norm2_g[l]), rw, ts)
        grank, aff, offs = _route(lg, ne, cap, tt)
        offs_flat = offs[:, :, :s // tt + 1].reshape(-1)
        nsub = 2 if (s // tt) % 2 == 0 else 1
        xg, wrep = _gather(offs_flat, u2, grank, aff, ne, cap, tt, nsub)
        nb = 4 if bsz % 4 == 0 else 1
        eo = _ffn(xg, wrep, expert_w_gate[l], expert_w_up[l], expert_w_down[l], nb, min(512, expert_w_gate.shape[3] // 2))
        h = _combine(offs_flat, grank, eo, h, mod, row(final_norm_g), tt, nsub, final=(l == depth - 1))
    return h
```

```python
import functools

import numpy as np
import jax
import jax.numpy as jnp
from jax import lax
from jax.experimental import pallas as pl
from jax.experimental.pallas import tpu as pltpu

F32 = jnp.float32
BF16 = jnp.bfloat16
I32 = jnp.int32

RMS_EPS = 1e-6
LN_EPS = 1e-5
FOURIER_GROUPS = 4
CAPACITY_FACTOR = 2
N_MOD = 6

LANES = 128
SUBLANES = 8
BF16_ROWS = 16
DFT_RADIX = 4
ROUTE_TILE = 256
MIXER_SUB_ROWS = 256
SLOT_WIN = 64
MIB = 1024 * 1024


def _cparams(sem, vmem_mib):
    return pltpu.CompilerParams(dimension_semantics=sem, vmem_limit_bytes=vmem_mib * MIB)


def _const_spec(shape):
    nd = len(shape)
    return pl.BlockSpec(shape, lambda *_: (0,) * nd)


def _rms_mod(x, g, shift, scale):
    return x * lax.rsqrt(jnp.mean(x * x, axis=-1, keepdims=True) + RMS_EPS) * (g * (1.0 + scale)) + shift


def _bdot(a, b):
    return jnp.dot(a, b, preferred_element_type=F32)


def _ada_kernel(c_ref, w_ref, b_ref, o_ref):
    c = c_ref[...]
    ca = c * jax.nn.sigmoid(c)
    o_ref[...] = _bdot(ca.astype(BF16), w_ref[...].astype(BF16)) + b_ref[...]


def _ada(c, w, b):
    bsz, d = c.shape
    n = w.shape[1]
    tn = min(n, 1536)
    return pl.pallas_call(
        _ada_kernel,
        grid=(n // tn,),
        in_specs=[_const_spec((bsz, d)),
                  pl.BlockSpec((d, tn), lambda j: (0, j)),
                  pl.BlockSpec((1, tn), lambda j: (0, j))],
        out_specs=pl.BlockSpec((bsz, tn), lambda j: (0, j)),
        out_shape=jax.ShapeDtypeStruct((bsz, n), F32),
        compiler_params=_cparams(("arbitrary",), 40),
        name="ada",
    )(c, w, b.reshape(1, n))


def _proj_cf_kernel(x_ref, mod_ref, g_ref, w_ref, b_ref, v_ref, f_ref, *, cw):
    u = _rms_mod(x_ref[0], g_ref[...], mod_ref[0, 0:1, :], mod_ref[0, 1:2, :])
    p = _bdot(u.astype(BF16), w_ref[...]) + b_ref[...]
    a = p[:, :cw]
    g = p[:, cw:2 * cw]
    v_ref[0] = (a * jax.nn.sigmoid(g)).astype(BF16)
    f_ref[0] = p[:, 2 * cw:].astype(BF16)


def _proj_cf(x, mod, g1, w_cf, b_cf, cw, fw, ts):
    bsz, s, d = x.shape
    n = w_cf.shape[1]
    tile = lambda m: pl.BlockSpec((1, ts, m), lambda b, j: (b, j, 0))
    return pl.pallas_call(
        functools.partial(_proj_cf_kernel, cw=cw),
        grid=(bsz, s // ts),
        in_specs=[tile(d),
                  pl.BlockSpec((1, N_MOD, d), lambda b, j: (b, 0, 0)),
                  _const_spec((1, d)),
                  _const_spec((d, n)),
                  _const_spec((1, n))],
        out_specs=[tile(cw), tile(fw)],
        out_shape=[jax.ShapeDtypeStruct((bsz, s, cw), BF16),
                   jax.ShapeDtypeStruct((bsz, s, fw), BF16)],
        compiler_params=_cparams(("arbitrary", "arbitrary"), 40),
        name="proj_cf",
    )(x, mod, g1, w_cf, b_cf)


def _group_block(fw):
    gd = fw // FOURIER_GROUPS
    return gd * max(1, min(FOURIER_GROUPS, LANES // gd))


def _dft_constants(s, fw):
    gd = fw // FOURIER_GROUPS
    q = s // DFT_RADIX
    j = np.arange(gd)
    ang = 2.0 * np.pi * np.outer(j, j) / gd
    scale = 1.0 / np.sqrt(float(s) * gd)
    wb = _group_block(fw)
    wr = np.zeros((wb, wb), np.float64)
    wi = np.zeros((wb, wb), np.float64)
    for g in range(wb // gd):
        sl = slice(g * gd, (g + 1) * gd)
        wr[sl, sl] = np.cos(ang) * scale
        wi[sl, sl] = -np.sin(ang) * scale
    wg = np.concatenate([wr, wi], axis=1).astype(np.float32)
    b = np.arange(q, dtype=np.int64)
    k = (DFT_RADIX * np.arange(q, dtype=np.int64)[None, :, None]
         + np.arange(DFT_RADIX, dtype=np.int64)[:, None, None])
    m = (k * b[None, None, :]) % s
    ang2 = 2.0 * np.pi * m.astype(np.float64) / s
    tab = np.concatenate([np.cos(ang2), np.sin(ang2)], axis=2).astype(np.float32)
    return wg, tab


def _fourier_kernel(f_ref, wg_ref, tab_ref, o_ref, a_ref, y_ref, *, q, fw, rc, pk):
    wg = wg_ref[...]
    wb = wg.shape[0]
    for c in range(q // rc):
        zr, zi = [], []
        for a in range(DFT_RADIX):
            fa = f_ref[0, a * q + c * rc:a * q + (c + 1) * rc, :]
            z = [_bdot(fa[:, g * wb:(g + 1) * wb], wg) for g in range(fw // wb)]
            zr.append(jnp.concatenate([t[:, :wb] for t in z], axis=1))
            zi.append(jnp.concatenate([t[:, wb:] for t in z], axis=1))
        er, ei = zr[0] + zr[2], zi[0] + zi[2]
        orr, oi = zr[1] + zr[3], zi[1] + zi[3]
        dr, di = zr[0] - zr[2], zi[0] - zi[2]
        pr, pi_ = zr[1] - zr[3], zi[1] - zi[3]
        ar = [er + orr, dr + pi_, er - orr, dr - pi_]
        ai = [ei + oi, di - pr, ei - oi, di + pr]
        for k1 in range(DFT_RADIX):
            a_ref[k1, c * rc:(c + 1) * rc, :] = ar[k1].astype(BF16)
            a_ref[k1, q + c * rc:q + (c + 1) * rc, :] = ai[k1].astype(BF16)
    for k1 in range(DFT_RADIX):
        y_ref[k1] = _bdot(tab_ref[k1], a_ref[k1]).astype(BF16)
    pt = DFT_RADIX * pk
    r_i = lax.broadcasted_iota(I32, (pt, pt), 0)
    c_i = lax.broadcasted_iota(I32, (pt, pt), 1)
    src = DFT_RADIX * jnp.bitwise_and(c_i, pk - 1) + jnp.right_shift(c_i, pk.bit_length() - 1)
    perm = jnp.where(r_i == src, 1.0, 0.0).astype(BF16)
    for t in range(q // pk):
        yc = jnp.concatenate([y_ref[k1, t * pk:(t + 1) * pk, :] for k1 in range(DFT_RADIX)], axis=0)
        o_ref[0, t * pt:(t + 1) * pt, :] = _bdot(perm, yc).astype(BF16)


def _fourier(f, wg, tab):
    bsz, s, fw = f.shape
    q = s // DFT_RADIX
    rc = min(q, 256)
    pk = min(q, LANES)
    assert pk & (pk - 1) == 0 and q % pk == 0
    return pl.pallas_call(
        functools.partial(_fourier_kernel, q=q, fw=fw, rc=rc, pk=pk),
        grid=(bsz,),
        in_specs=[pl.BlockSpec((1, s, fw), lambda b: (b, 0, 0)),
                  pl.BlockSpec(wg.shape, lambda b: (0, 0), pipeline_mode=pl.Buffered(1)),
                  pl.BlockSpec((DFT_RADIX, q, 2 * q), lambda b: (0, 0, 0), pipeline_mode=pl.Buffered(1))],
        out_specs=pl.BlockSpec((1, s, fw), lambda b: (b, 0, 0)),
        out_shape=jax.ShapeDtypeStruct((bsz, s, fw), BF16),
        scratch_shapes=[pltpu.VMEM((DFT_RADIX, 2 * q, fw), BF16),
                        pltpu.VMEM((DFT_RADIX, q, fw), BF16)],
        compiler_params=_cparams(("arbitrary",), 56),
        name="fourier",
    )(f, wg, tab)


def _mixer_kernel(x_ref, mod_ref, g1_ref, wg_ref, bg_ref, v_ref, dww_ref, dwb_ref, lng_ref, lnb_ref,
                  wco_ref, bco_ref, fr_ref, wf_ref, bf_ref, wo_ref, bo_ref, g2_ref, rw_ref,
                  h_ref, u2_ref, lg_ref, vwin_ref, *, ts, tsb, s, d, kw, halo):
    j = pl.program_id(1)
    nj = pl.num_programs(1)
    s0 = pl.multiple_of(j * ts, ts)
    pad = kw // 2
    mod = mod_ref[0]

    vwin_ref[halo:halo + ts, :] = v_ref[0, pl.ds(s0, ts), :].astype(F32)
    lo = pl.multiple_of(jnp.maximum(s0 - halo, 0), halo)
    vwin_ref[0:halo, :] = jnp.where(j > 0, v_ref[0, pl.ds(lo, halo), :].astype(F32), 0.0)
    hi = pl.multiple_of(jnp.minimum(s0 + ts, s - halo), halo)
    vwin_ref[halo + ts:halo + ts + halo, :] = jnp.where(
        j < nj - 1, v_ref[0, pl.ds(hi, halo), :].astype(F32), 0.0)

    first = halo - pad
    for sb in range(ts // tsb):
        r0 = sb * tsb
        rows = slice(r0, r0 + tsb)
        x = x_ref[0, rows, :]
        u = _rms_mod(x, g1_ref[...], mod[0:1], mod[1:2])
        gates = jax.nn.sigmoid(_bdot(u.astype(BF16), wg_ref[...]) + bg_ref[...])
        acc = None
        for r in range(SUBLANES):
            part = None
            for qq in range((first + kw + SUBLANES - 1) // SUBLANES):
                o = SUBLANES * qq + r
                if first <= o < first + kw:
                    win = vwin_ref[r0 + SUBLANES * qq:r0 + SUBLANES * qq + tsb + SUBLANES, :]
                    term = win * dww_ref[o - first:o - first + 1, :]
                    part = term if part is None else part + term
            if part is None:
                continue
            shifted = part[:tsb] if r == 0 else pltpu.roll(part, tsb + SUBLANES - r, axis=0)[:tsb]
            acc = shifted + dwb_ref[...] if acc is None else acc + shifted
        mu = jnp.mean(acc, axis=-1, keepdims=True)
        xc = acc - mu
        var = jnp.mean(xc * xc, axis=-1, keepdims=True)
        cv = xc * lax.rsqrt(var + LN_EPS) * lng_ref[...] + lnb_ref[...]
        cv = cv * jax.nn.sigmoid(cv)
        y_conv = _bdot(cv.astype(BF16), wco_ref[...]) + bco_ref[...]
        y_four = _bdot(fr_ref[0, rows, :], wf_ref[...]) + bf_ref[...]
        merged = gates[:, :d] * y_conv + gates[:, d:] * y_four
        o_proj = _bdot(merged.astype(BF16), wo_ref[...]) + bo_ref[...]
        h = x + mod[2:3] * o_proj
        h_ref[0, rows, :] = h
        u2b = _rms_mod(h, g2_ref[...], mod[3:4], mod[4:5]).astype(BF16)
        u2_ref[0, rows, :] = u2b
        lg_ref[0, rows, :] = _bdot(u2b, rw_ref[...])


def _mixer(x, mod, g1, w_g, b_g, v, dww, dwb, lng, lnb, wco, bco, fr, wf, bf, wo, bo, g2, rw, ts):
    bsz, s, d = x.shape
    cw = v.shape[2]
    fw = fr.shape[2]
    kw = dww.shape[0]
    halo = BF16_ROWS
    tsb = min(ts, MIXER_SUB_ROWS)
    assert kw // 2 <= halo and s % ts == 0 and ts % tsb == 0 and tsb % halo == 0
    tile = lambda n: pl.BlockSpec((1, ts, n), lambda b, j: (b, j, 0))
    return pl.pallas_call(
        functools.partial(_mixer_kernel, ts=ts, tsb=tsb, s=s, d=d, kw=kw, halo=halo),
        grid=(bsz, s // ts),
        in_specs=[tile(d),
                  pl.BlockSpec((1, N_MOD, d), lambda b, j: (b, 0, 0)),
                  _const_spec((1, d)),
                  _const_spec((d, 2 * d)), _const_spec((1, 2 * d)),
                  pl.BlockSpec((1, s, cw), lambda b, j: (b, 0, 0)),
                  _const_spec((kw, cw)), _const_spec((1, cw)), _const_spec((1, cw)), _const_spec((1, cw)),
                  _const_spec((cw, d)), _const_spec((1, d)),
                  tile(fw),
                  _const_spec((fw, d)), _const_spec((1, d)),
                  _const_spec((d, d)), _const_spec((1, d)),
                  _const_spec((1, d)),
                  _const_spec((d, LANES))],
        out_specs=[tile(d), tile(d), tile(LANES)],
        out_shape=[jax.ShapeDtypeStruct((bsz, s, d), F32),
                   jax.ShapeDtypeStruct((bsz, s, d), BF16),
                   jax.ShapeDtypeStruct((bsz, s, LANES), F32)],
        scratch_shapes=[pltpu.VMEM((ts + 2 * halo, cw), F32)],
        compiler_params=_cparams(("arbitrary", "arbitrary"), 56),
        name="mixer",
    )(x, mod, g1, w_g, b_g, v, dww, dwb, lng, lnb, wco, bco, fr, wf, bf, wo, bo, g2, rw)


def _route_kernel(lg_ref, grank_ref, aff_ref, offs_ref, *, ne, cap, s, tt):
    lt = lg_ref[0].T[:ne, :]
    m = jnp.max(lt, axis=0, keepdims=True)
    e = jnp.exp(lt - m)
    aff = e / jnp.sum(e, axis=0, keepdims=True)
    aff_ref[0] = aff

    def search(i, thr_bits):
        cand = thr_bits | jnp.left_shift(jnp.int32(1), 30 - i)
        cnt = jnp.sum(jnp.where(aff >= pltpu.bitcast(cand, F32), 1.0, 0.0), axis=1, keepdims=True)
        return jnp.where(cnt >= cap, cand, thr_bits)

    thr = pltpu.bitcast(lax.fori_loop(0, 31, search, jnp.zeros((ne, 1), I32)), F32)
    gt = aff > thr
    tie = aff == thr
    need = cap - jnp.sum(jnp.where(gt, 1.0, 0.0), axis=1, keepdims=True)

    r_i = lax.broadcasted_iota(I32, (tt, tt), 0)
    c_i = lax.broadcasted_iota(I32, (tt, tt), 1)
    upper = jnp.where(r_i < c_i, 1.0, 0.0).astype(BF16)
    offs_ref[0] = jnp.zeros(offs_ref.shape[1:], I32)
    carry_tie = jnp.zeros((ne, 1), F32)
    carry_sel = jnp.zeros((ne, 1), F32)
    for j in range(s // tt):
        sl = slice(j * tt, (j + 1) * tt)
        tie_f = jnp.where(tie[:, sl], 1.0, 0.0)
        tie_rank = _bdot(tie_f.astype(BF16), upper) + carry_tie
        carry_tie = carry_tie + jnp.sum(tie_f, axis=1, keepdims=True)
        sel_f = jnp.where(gt[:, sl], 1.0, jnp.where(tie_rank < need, tie_f, 0.0))
        rank = _bdot(sel_f.astype(BF16), upper) + carry_sel
        offs_ref[0, :, j:j + 1] = carry_sel.astype(I32)
        carry_sel = carry_sel + jnp.sum(sel_f, axis=1, keepdims=True)
        grank_ref[0, :, sl] = jnp.where(sel_f > 0.5, rank.astype(I32), -1)
    offs_ref[0, :, s // tt:s // tt + 1] = carry_sel.astype(I32)


def _route(lg, ne, cap, tt):
    bsz, s, _ = lg.shape
    return pl.pallas_call(
        functools.partial(_route_kernel, ne=ne, cap=cap, s=s, tt=tt),
        grid=(bsz,),
        in_specs=[pl.BlockSpec((1, s, LANES), lambda b: (b, 0, 0))],
        out_specs=[pl.BlockSpec((1, ne, s), lambda b: (b, 0, 0)),
                   pl.BlockSpec((1, ne, s), lambda b: (b, 0, 0)),
                   pl.BlockSpec((1, ne, LANES), lambda b: (b, 0, 0))],
        out_shape=[jax.ShapeDtypeStruct((bsz, ne, s), I32),
                   jax.ShapeDtypeStruct((bsz, ne, s), F32),
                   jax.ShapeDtypeStruct((bsz, ne, LANES), I32)],
        compiler_params=_cparams(("arbitrary",), 40),
        name="route",
    )(lg)


def _round_plan(offs_ref, b, j, ne, nt, cap):
    base = [(b * ne + e) * (nt + 1) + j for e in range(ne)]
    off = [offs_ref[base[e]] for e in range(ne)]
    end = [offs_ref[base[e] + 1] for e in range(ne)]
    w0 = [(off[e] // BF16_ROWS) * BF16_ROWS for e in range(ne)]
    nr = jnp.int32(0)
    for e in range(ne):
        nr_e = jnp.where(end[e] > off[e], (end[e] - w0[e] + SLOT_WIN - 1) // SLOT_WIN, 0)
        nr = jnp.maximum(nr, nr_e)
    return w0, nr


def _one_hot_windows(gr, w0, r, ne, tt, cap):
    srow = lax.broadcasted_iota(I32, (SLOT_WIN, tt), 0)
    starts, masks = [], []
    for e in range(ne):
        lo = w0[e] + SLOT_WIN * r
        start = pl.multiple_of(jnp.minimum(lo, cap - SLOT_WIN), BF16_ROWS)
        g = gr[e:e + 1, :]
        g = jnp.where(g >= lo, g, -1)
        masks.append((g - start) == srow)
        starts.append(start)
    return starts, masks


def _gather_kernel(offs_ref, u2_ref, grank_ref, aff_ref, xg_ref, wrep_ref, *, ne, nt, tt, cap, nsub):
    b = pl.program_id(0)
    j = pl.program_id(1)

    @pl.when(j == 0)
    def _():
        xg_ref[...] = jnp.zeros(xg_ref.shape, BF16)
        wrep_ref[...] = jnp.zeros(wrep_ref.shape, F32)

    plans = [_round_plan(offs_ref, b, j * nsub + sub, ne, nt, cap) for sub in range(nsub)]

    def do_round(sub, r):
        cols = slice(sub * tt, (sub + 1) * tt)
        starts, masks = _one_hot_windows(grank_ref[0, :, cols], plans[sub][0], r, ne, tt, cap)
        pcat = jnp.concatenate([jnp.where(mk, 1.0, 0.0).astype(BF16) for mk in masks], axis=0)
        xw = _bdot(pcat, u2_ref[0, cols, :]).astype(BF16)
        af = aff_ref[0, :, cols]
        for e in range(ne):
            rows = pl.ds(starts[e], SLOT_WIN)
            xg_ref[0, e, rows, :] = xg_ref[0, e, rows, :] + xw[e * SLOT_WIN:(e + 1) * SLOT_WIN]
            ws = jnp.sum(jnp.where(masks[e], af[e:e + 1, :], 0.0), axis=1, keepdims=True)
            wrep_ref[0, e, rows, :] = wrep_ref[0, e, rows, :] + jnp.broadcast_to(ws, (SLOT_WIN, LANES))

    for sub in range(nsub):
        do_round(sub, 0)
    for sub in range(nsub):
        def extra(r, carry, sub=sub):
            do_round(sub, r)
            return carry
        lax.fori_loop(1, plans[sub][1], extra, 0)


def _gather(offs, u2, grank, aff, ne, cap, tt, nsub):
    bsz, s, d = u2.shape
    nt = s // tt
    tg = nsub * tt
    grid_spec = pltpu.PrefetchScalarGridSpec(
        num_scalar_prefetch=1,
        grid=(bsz, nt // nsub),
        in_specs=[pl.BlockSpec((1, tg, d), lambda b, j, o: (b, j, 0)),
                  pl.BlockSpec((1, ne, tg), lambda b, j, o: (b, 0, j)),
                  pl.BlockSpec((1, ne, tg), lambda b, j, o: (b, 0, j))],
        out_specs=[pl.BlockSpec((1, ne, cap, d), lambda b, j, o: (b, 0, 0, 0)),
                   pl.BlockSpec((1, ne, cap, LANES), lambda b, j, o: (b, 0, 0, 0))],
    )
    return pl.pallas_call(
        functools.partial(_gather_kernel, ne=ne, nt=nt, tt=tt, cap=cap, nsub=nsub),
        grid_spec=grid_spec,
        out_shape=[jax.ShapeDtypeStruct((bsz, ne, cap, d), BF16),
                   jax.ShapeDtypeStruct((bsz, ne, cap, LANES), F32)],
        compiler_params=_cparams(("arbitrary", "arbitrary"), 56),
        name="gather",
    )(offs, u2, grank, aff)


def _ffn_kernel(x_ref, wg_ref, wu_ref, wd_ref, wrep_ref, o_ref, acc_ref, *, nb, cap, d):
    f = pl.program_id(2)
    last = pl.num_programs(2) - 1
    wg = wg_ref[0].astype(BF16)
    wu = wu_ref[0].astype(BF16)
    wd = wd_ref[0].astype(BF16)

    def partial_out(i):
        x = x_ref[i, 0]
        hg = _bdot(x, wg)
        hu = _bdot(x, wu)
        act = (hg * jax.nn.sigmoid(hg) * hu).astype(BF16)
        return _bdot(act, wd)

    @pl.when(f == 0)
    def _():
        for i in range(nb):
            acc_ref[i] = partial_out(i)

    @pl.when(jnp.logical_and(f > 0, f < last))
    def _():
        for i in range(nb):
            acc_ref[i] += partial_out(i)

    @pl.when(f == last)
    def _():
        for i in range(nb):
            tot = acc_ref[i] + partial_out(i)
            w = wrep_ref[i, 0]
            for c in range(d // LANES):
                cols = slice(c * LANES, (c + 1) * LANES)
                o_ref[i, 0, :, cols] = (tot[:, cols] * w).astype(BF16)


def _ffn(xg, wrep, w_gate, w_up, w_down, nb, tf):
    bsz, ne, cap, d = xg.shape
    ff = w_gate.shape[2]
    assert ff % tf == 0 and ff // tf >= 2
    return pl.pallas_call(
        functools.partial(_ffn_kernel, nb=nb, cap=cap, d=d),
        grid=(ne, bsz // nb, ff // tf),
        in_specs=[pl.BlockSpec((nb, 1, cap, d), lambda e, m, f: (m, e, 0, 0)),
                  pl.BlockSpec((1, d, tf), lambda e, m, f: (e, 0, f)),
                  pl.BlockSpec((1, d, tf), lambda e, m, f: (e, 0, f)),
                  pl.BlockSpec((1, tf, d), lambda e, m, f: (e, f, 0)),
                  pl.BlockSpec((nb, 1, cap, LANES), lambda e, m, f: (m, e, 0, 0))],
        out_specs=pl.BlockSpec((nb, 1, cap, d), lambda e, m, f: (m, e, 0, 0)),
        out_shape=jax.ShapeDtypeStruct((bsz, ne, cap, d), BF16),
        scratch_shapes=[pltpu.VMEM((nb, cap, d), F32)],
        compiler_params=_cparams(("arbitrary", "arbitrary", "arbitrary"), 56),
        name="ffn",
    )(xg, w_gate, w_up, w_down, wrep)


def _combine_kernel(offs_ref, grank_ref, eo_ref, h_ref, mod_ref, gf_ref, o_ref, y_ref, *,
                    ne, nt, tt, cap, nsub, final):
    b = pl.program_id(0)
    j = pl.program_id(1)
    plans = [_round_plan(offs_ref, b, j * nsub + sub, ne, nt, cap) for sub in range(nsub)]

    def round_sum(sub, r):
        cols = slice(sub * tt, (sub + 1) * tt)
        starts, masks = _one_hot_windows(grank_ref[0, :, cols], plans[sub][0], r, ne, tt, cap)
        pcat = jnp.concatenate([jnp.where(mk, 1.0, 0.0).astype(BF16) for mk in masks], axis=0)
        ocat = jnp.concatenate([eo_ref[0, e, pl.ds(starts[e], SLOT_WIN), :] for e in range(ne)], axis=0)
        return lax.dot_general(pcat, ocat, (((0,), (0,)), ((), ())), preferred_element_type=F32)

    for sub in range(nsub):
        y_ref[sub * tt:(sub + 1) * tt, :] = round_sum(sub, 0)
    for sub in range(nsub):
        def extra(r, carry, sub=sub):
            y_ref[sub * tt:(sub + 1) * tt, :] += round_sum(sub, r)
            return carry
        lax.fori_loop(1, plans[sub][1], extra, 0)
    hout = h_ref[0] + mod_ref[0, 5:6, :] * y_ref[...]
    if final:
        hout = hout * lax.rsqrt(jnp.mean(hout * hout, axis=-1, keepdims=True) + RMS_EPS) * gf_ref[...]
    o_ref[0] = hout


def _combine(offs, grank, eo, h, mod, gf, tt, nsub, final):
    bsz, s, d = h.shape
    ne, cap = eo.shape[1], eo.shape[2]
    nt = s // tt
    tg = nsub * tt
    grid_spec = pltpu.PrefetchScalarGridSpec(
        num_scalar_prefetch=1,
        grid=(bsz, nt // nsub),
        in_specs=[pl.BlockSpec((1, ne, tg), lambda b, j, o: (b, 0, j)),
                  pl.BlockSpec((1, ne, cap, d), lambda b, j, o: (b, 0, 0, 0)),
                  pl.BlockSpec((1, tg, d), lambda b, j, o: (b, j, 0)),
                  pl.BlockSpec((1, N_MOD, d), lambda b, j, o: (b, 0, 0)),
                  pl.BlockSpec((1, d), lambda b, j, o: (0, 0))],
        out_specs=pl.BlockSpec((1, tg, d), lambda b, j, o: (b, j, 0)),
        scratch_shapes=[pltpu.VMEM((tg, d), F32)],
    )
    return pl.pallas_call(
        functools.partial(_combine_kernel, ne=ne, nt=nt, tt=tt, cap=cap, nsub=nsub, final=final),
        grid_spec=grid_spec,
        out_shape=jax.ShapeDtypeStruct((bsz, s, d), F32),
        compiler_params=_cparams(("arbitrary", "arbitrary"), 56),
        name="combine",
    )(offs, grank, eo, h, mod, gf)


def kernel(x, c, ada_w, ada_b, norm1_g, w_in, b_in, conv_dw_w, conv_dw_b, conv_ln_g, conv_ln_b,
           conv_w_out, conv_b_out, fourier_w, fourier_b, w_out, b_out, norm2_g, router_w,
           expert_w_gate, expert_w_up, expert_w_down, final_norm_g):
    bsz, s, d = x.shape
    depth = ada_w.shape[0]
    cw = conv_dw_w.shape[2]
    fw = fourier_w.shape[1]
    ne = router_w.shape[2]
    cap = CAPACITY_FACTOR * s // ne
    tt = min(ROUTE_TILE, s)
    ts = min(512, s)
    assert s % DFT_RADIX == 0 and s % tt == 0 and cap % BF16_ROWS == 0 and cap >= SLOT_WIN and ne <= LANES
    wg_np, tab_np = _dft_constants(s, fw)
    dft_wg = jnp.asarray(wg_np).astype(BF16)
    dft_tab = jnp.asarray(tab_np).astype(BF16)
    row = lambda a: a.reshape(1, -1)

    h = x
    for l in range(depth):
        mod = _ada(c, ada_w[l], ada_b[l]).reshape(bsz, N_MOD, d)
        c1 = 2 * cw + fw
        v, f = _proj_cf(h, mod, row(norm1_g[l]), w_in[l][:, :c1].astype(BF16), row(b_in[l][:c1]), cw, fw, ts)
        fr = _fourier(f, dft_wg, dft_tab)
        rw = jnp.zeros((d, LANES), BF16).at[:, :ne].set(router_w[l].astype(BF16))
        h, u2, lg = _mixer(h, mod, row(norm1_g[l]), w_in[l][:, c1:].astype(BF16), row(b_in[l][c1:]), v,
                           conv_dw_w[l], row(conv_dw_b[l]), row(conv_ln_g[l]), row(conv_ln_b[l]),
                           conv_w_out[l].astype(BF16), row(conv_b_out[l]), fr,
                           fourier_w[l].astype(BF16), row(fourier_b[l]),
                           w_out[l].astype(BF16), row(b_out[l]), row(norm2_g[l]), rw, ts)
        grank, aff, offs = _route(lg, ne, cap, tt)
        offs_flat = offs[:, :, :s // tt + 1].reshape(-1)
        nsub = 2 if (s // tt) % 2 == 0 else 1
        xg, wrep = _gather(offs_flat, u2, grank, aff, ne, cap, tt, nsub)
        nb = 4 if bsz % 4 == 0 else 1
        eo = _ffn(xg, wrep, expert_w_gate[l], expert_w_up[l], expert_w_down[l], nb, min(512, expert_w_gate.shape[3] // 2))
        h = _combine(offs_flat, grank, eo, h, mod, row(final_norm_g), tt, nsub, final=(l == depth - 1))
    return h
```

```python
import functools

import numpy as np
import jax
import jax.numpy as jnp
from jax import lax
from jax.experimental import pallas as pl
from jax.experimental.pallas import tpu as pltpu

F32 = jnp.float32
BF16 = jnp.bfloat16
I32 = jnp.int32

RMS_EPS = 1e-6
LN_EPS = 1e-5
FOURIER_GROUPS = 4
CAPACITY_FACTOR = 2
N_MOD = 6

LANES = 128
SUBLANES = 8
BF16_ROWS = 16
DFT_RADIX = 4
ROUTE_TILE = 256
SLOT_WIN = 64
MIB = 1024 * 1024


def _cparams(sem, vmem_mib):
    return pltpu.CompilerParams(dimension_semantics=sem, vmem_limit_bytes=vmem_mib * MIB)


def _const_spec(shape):
    nd = len(shape)
    return pl.BlockSpec(shape, lambda *_: (0,) * nd)


def _rms_mod(x, g, shift, scale):
    return x * lax.rsqrt(jnp.mean(x * x, axis=-1, keepdims=True) + RMS_EPS) * (g * (1.0 + scale)) + shift


def _bdot(a, b):
    return jnp.dot(a, b, preferred_element_type=F32)


def _ada_kernel(c_ref, w_ref, b_ref, o_ref):
    c = c_ref[...]
    ca = c * jax.nn.sigmoid(c)
    o_ref[...] = _bdot(ca.astype(BF16), w_ref[...].astype(BF16)) + b_ref[...]


def _ada(c, w, b):
    bsz, d = c.shape
    n = w.shape[1]
    tn = min(n, 1536)
    return pl.pallas_call(
        _ada_kernel,
        grid=(n // tn,),
        in_specs=[_const_spec((bsz, d)),
                  pl.BlockSpec((d, tn), lambda j: (0, j)),
                  pl.BlockSpec((1, tn), lambda j: (0, j))],
        out_specs=pl.BlockSpec((bsz, tn), lambda j: (0, j)),
        out_shape=jax.ShapeDtypeStruct((bsz, n), F32),
        compiler_params=_cparams(("arbitrary",), 40),
        name="ada",
    )(c, w, b.reshape(1, n))


def _proj_cf_kernel(x_ref, mod_ref, g_ref, w_ref, b_ref, v_ref, f_ref, *, cw):
    u = _rms_mod(x_ref[0], g_ref[...], mod_ref[0, 0:1, :], mod_ref[0, 1:2, :])
    p = _bdot(u.astype(BF16), w_ref[...]) + b_ref[...]
    a = p[:, :cw]
    g = p[:, cw:2 * cw]
    v_ref[0] = (a * jax.nn.sigmoid(g)).astype(BF16)
    f_ref[0] = p[:, 2 * cw:].astype(BF16)


def _proj_cf(x, mod, g1, w_cf, b_cf, cw, fw, ts):
    bsz, s, d = x.shape
    n = w_cf.shape[1]
    tile = lambda m: pl.BlockSpec((1, ts, m), lambda b, j: (b, j, 0))
    return pl.pallas_call(
        functools.partial(_proj_cf_kernel, cw=cw),
        grid=(bsz, s // ts),
        in_specs=[tile(d),
                  pl.BlockSpec((1, N_MOD, d), lambda b, j: (b, 0, 0)),
                  _const_spec((1, d)),
                  _const_spec((d, n)),
                  _const_spec((1, n))],
        out_specs=[tile(cw), tile(fw)],
        out_shape=[jax.ShapeDtypeStruct((bsz, s, cw), BF16),
                   jax.ShapeDtypeStruct((bsz, s, fw), BF16)],
        compiler_params=_cparams(("arbitrary", "arbitrary"), 40),
        name="proj_cf",
    )(x, mod, g1, w_cf, b_cf)


def _group_block(fw):
    gd = fw // FOURIER_GROUPS
    return gd * max(1, min(FOURIER_GROUPS, LANES // gd))


def _dft_constants(s, fw):
    gd = fw // FOURIER_GROUPS
    q = s // DFT_RADIX
    j = np.arange(gd)
    ang = 2.0 * np.pi * np.outer(j, j) / gd
    scale = 1.0 / np.sqrt(float(s) * gd)
    wb = _group_block(fw)
    wr = np.zeros((wb, wb), np.float64)
    wi = np.zeros((wb, wb), np.float64)
    for g in range(wb // gd):
        sl = slice(g * gd, (g + 1) * gd)
        wr[sl, sl] = np.cos(ang) * scale
        wi[sl, sl] = -np.sin(ang) * scale
    wg = np.concatenate([wr, wi], axis=1).astype(np.float32)
    b = np.arange(q, dtype=np.int64)
    k = (DFT_RADIX * np.arange(q, dtype=np.int64)[None, :, None]
         + np.arange(DFT_RADIX, dtype=np.int64)[:, None, None])
    m = (k * b[None, None, :]) % s
    ang2 = 2.0 * np.pi * m.astype(np.float64) / s
    tab = np.concatenate([np.cos(ang2), np.sin(ang2)], axis=2).astype(np.float32)
    return wg, tab


def _fourier_kernel(f_ref, wg_ref, tab_ref, o_ref, a_ref, y_ref, *, q, fw, rc, pk):
    wg = wg_ref[...]
    wb = wg.shape[0]
    for c in range(q // rc):
        zr, zi = [], []
        for a in range(DFT_RADIX):
            fa = f_ref[0, a * q + c * rc:a * q + (c + 1) * rc, :]
            z = [_bdot(fa[:, g * wb:(g + 1) * wb], wg) for g in range(fw // wb)]
            zr.append(jnp.concatenate([t[:, :wb] for t in z], axis=1))
            zi.append(jnp.concatenate([t[:, wb:] for t in z], axis=1))
        er, ei = zr[0] + zr[2], zi[0] + zi[2]
        orr, oi = zr[1] + zr[3], zi[1] + zi[3]
        dr, di = zr[0] - zr[2], zi[0] - zi[2]
        pr, pi_ = zr[1] - zr[3], zi[1] - zi[3]
        ar = [er + orr, dr + pi_, er - orr, dr - pi_]
        ai = [ei + oi, di - pr, ei - oi, di + pr]
        for k1 in range(DFT_RADIX):
            a_ref[k1, c * rc:(c + 1) * rc, :] = ar[k1].astype(BF16)
            a_ref[k1, q + c * rc:q + (c + 1) * rc, :] = ai[k1].astype(BF16)
    for k1 in range(DFT_RADIX):
        y_ref[k1] = _bdot(tab_ref[k1], a_ref[k1]).astype(BF16)
    pt = DFT_RADIX * pk
    r_i = lax.broadcasted_iota(I32, (pt, pt), 0)
    c_i = lax.broadcasted_iota(I32, (pt, pt), 1)
    src = DFT_RADIX * jnp.bitwise_and(c_i, pk - 1) + jnp.right_shift(c_i, pk.bit_length() - 1)
    perm = jnp.where(r_i == src, 1.0, 0.0).astype(BF16)
    for t in range(q // pk):
        yc = jnp.concatenate([y_ref[k1, t * pk:(t + 1) * pk, :] for k1 in range(DFT_RADIX)], axis=0)
        o_ref[0, t * pt:(t + 1) * pt, :] = _bdot(perm, yc).astype(BF16)


def _fourier(f, wg, tab):
    bsz, s, fw = f.shape
    q = s // DFT_RADIX
    rc = min(q, 256)
    pk = min(q, LANES)
    assert pk & (pk - 1) == 0 and q % pk == 0
    return pl.pallas_call(
        functools.partial(_fourier_kernel, q=q, fw=fw, rc=rc, pk=pk),
        grid=(bsz,),
        in_specs=[pl.BlockSpec((1, s, fw), lambda b: (b, 0, 0)),
                  pl.BlockSpec(wg.shape, lambda b: (0, 0), pipeline_mode=pl.Buffered(1)),
                  pl.BlockSpec((DFT_RADIX, q, 2 * q), lambda b: (0, 0, 0), pipeline_mode=pl.Buffered(1))],
        out_specs=pl.BlockSpec((1, s, fw), lambda b: (b, 0, 0)),
        out_shape=jax.ShapeDtypeStruct((bsz, s, fw), BF16),
        scratch_shapes=[pltpu.VMEM((DFT_RADIX, 2 * q, fw), BF16),
                        pltpu.VMEM((DFT_RADIX, q, fw), BF16)],
        compiler_params=_cparams(("arbitrary",), 56),
        name="fourier",
    )(f, wg, tab)


def _mixer_kernel(x_ref, mod_ref, g1_ref, wg_ref, bg_ref, v_ref, dww_ref, dwb_ref, lng_ref, lnb_ref,
                  wco_ref, bco_ref, fr_ref, wf_ref, bf_ref, wo_ref, bo_ref, g2_ref, rw_ref,
                  h_ref, u2_ref, lg_ref, vwin_ref, *, ts, tsb, s, d, kw, halo):
    j = pl.program_id(1)
    nj = pl.num_programs(1)
    s0 = pl.multiple_of(j * ts, ts)
    pad = kw // 2
    mod = mod_ref[0]

    vwin_ref[halo:halo + ts, :] = v_ref[0, pl.ds(s0, ts), :].astype(F32)
    lo = pl.multiple_of(jnp.maximum(s0 - halo, 0), halo)
    vwin_ref[0:halo, :] = jnp.where(j > 0, v_ref[0, pl.ds(lo, halo), :].astype(F32), 0.0)
    hi = pl.multiple_of(jnp.minimum(s0 + ts, s - halo), halo)
    vwin_ref[halo + ts:halo + ts + halo, :] = jnp.where(
        j < nj - 1, v_ref[0, pl.ds(hi, halo), :].astype(F32), 0.0)

    first = halo - pad
    for sb in range(ts // tsb):
        r0 = sb * tsb
        rows = slice(r0, r0 + tsb)
        x = x_ref[0, rows, :]
        u = _rms_mod(x, g1_ref[...], mod[0:1], mod[1:2])
        gates = jax.nn.sigmoid(_bdot(u.astype(BF16), wg_ref[...]) + bg_ref[...])
        acc = None
        for r in range(SUBLANES):
            part = None
            for qq in range((first + kw + SUBLANES - 1) // SUBLANES):
                o = SUBLANES * qq + r
                if first <= o < first + kw:
                    win = vwin_ref[r0 + SUBLANES * qq:r0 + SUBLANES * qq + tsb + SUBLANES, :]
                    term = win * dww_ref[o - first:o - first + 1, :]
                    part = term if part is None else part + term
            if part is None:
                continue
            shifted = part[:tsb] if r == 0 else pltpu.roll(part, tsb + SUBLANES - r, axis=0)[:tsb]
            acc = shifted + dwb_ref[...] if acc is None else acc + shifted
        mu = jnp.mean(acc, axis=-1, keepdims=True)
        xc = acc - mu
        var = jnp.mean(xc * xc, axis=-1, keepdims=True)
        cv = xc * lax.rsqrt(var + LN_EPS) * lng_ref[...] + lnb_ref[...]
        cv = cv * jax.nn.sigmoid(cv)
        y_conv = _bdot(cv.astype(BF16), wco_ref[...]) + bco_ref[...]
        y_four = _bdot(fr_ref[0, rows, :], wf_ref[...]) + bf_ref[...]
        merged = gates[:, :d] * y_conv + gates[:, d:] * y_four
        o_proj = _bdot(merged.astype(BF16), wo_ref[...]) + bo_ref[...]
        h = x + mod[2:3] * o_proj
        h_ref[0, rows, :] = h
        u2b = _rms_mod(h, g2_ref[...], mod[3:4], mod[4:5]).astype(BF16)
        u2_ref[0, rows, :] = u2b
        lg_ref[0, rows, :] = _bdot(u2b, rw_ref[...])


def _mixer(x, mod, g1, w_g, b_g, v, dww, dwb, lng, lnb, wco, bco, fr, wf, bf, wo, bo, g2, rw, ts):
    bsz, s, d = x.shape
    cw = v.shape[2]
    fw = fr.shape[2]
    kw = dww.shape[0]
    halo = BF16_ROWS
    tsb = ts
    assert kw // 2 <= halo and s % ts == 0 and ts % tsb == 0 and tsb % halo == 0
    tile = lambda n: pl.BlockSpec((1, ts, n), lambda b, j: (b, j, 0))
    return pl.pallas_call(
        functools.partial(_mixer_kernel, ts=ts, tsb=tsb, s=s, d=d, kw=kw, halo=halo),
        grid=(bsz, s // ts),
        in_specs=[tile(d),
                  pl.BlockSpec((1, N_MOD, d), lambda b, j: (b, 0, 0)),
                  _const_spec((1, d)),
                  _const_spec((d, 2 * d)), _const_spec((1, 2 * d)),
                  pl.BlockSpec((1, s, cw), lambda b, j: (b, 0, 0)),
                  _const_spec((kw, cw)), _const_spec((1, cw)), _const_spec((1, cw)), _const_spec((1, cw)),
                  _const_spec((cw, d)), _const_spec((1, d)),
                  tile(fw),
                  _const_spec((fw, d)), _const_spec((1, d)),
                  _const_spec((d, d)), _const_spec((1, d)),
                  _const_spec((1, d)),
                  _const_spec((d, LANES))],
        out_specs=[tile(d), tile(d), tile(LANES)],
        out_shape=[jax.ShapeDtypeStruct((bsz, s, d), F32),
                   jax.ShapeDtypeStruct((bsz, s, d), BF16),
                   jax.ShapeDtypeStruct((bsz, s, LANES), F32)],
        scratch_shapes=[pltpu.VMEM((ts + 2 * halo, cw), F32)],
        compiler_params=_cparams(("arbitrary", "arbitrary"), 56),
        name="mixer",
    )(x, mod, g1, w_g, b_g, v, dww, dwb, lng, lnb, wco, bco, fr, wf, bf, wo, bo, g2, rw)


def _route_kernel(lg_ref, grank_ref, aff_ref, offs_ref, *, ne, cap, s, tt):
    lt = lg_ref[0].T[:ne, :]
    m = jnp.max(lt, axis=0, keepdims=True)
    e = jnp.exp(lt - m)
    aff = e / jnp.sum(e, axis=0, keepdims=True)
    aff_ref[0] = aff

    def search(i, thr_bits):
        cand = thr_bits | jnp.left_shift(jnp.int32(1), 30 - i)
        cnt = jnp.sum(jnp.where(aff >= pltpu.bitcast(cand, F32), 1.0, 0.0), axis=1, keepdims=True)
        return jnp.where(cnt >= cap, cand, thr_bits)

    thr = pltpu.bitcast(lax.fori_loop(0, 31, search, jnp.zeros((ne, 1), I32)), F32)
    gt = aff > thr
    tie = aff == thr
    need = cap - jnp.sum(jnp.where(gt, 1.0, 0.0), axis=1, keepdims=True)

    r_i = lax.broadcasted_iota(I32, (tt, tt), 0)
    c_i = lax.broadcasted_iota(I32, (tt, tt), 1)
    upper = jnp.where(r_i < c_i, 1.0, 0.0).astype(BF16)
    offs_ref[0] = jnp.zeros(offs_ref.shape[1:], I32)
    carry_tie = jnp.zeros((ne, 1), F32)
    carry_sel = jnp.zeros((ne, 1), F32)
    for j in range(s // tt):
        sl = slice(j * tt, (j + 1) * tt)
        tie_f = jnp.where(tie[:, sl], 1.0, 0.0)
        tie_rank = _bdot(tie_f.astype(BF16), upper) + carry_tie
        carry_tie = carry_tie + jnp.sum(tie_f, axis=1, keepdims=True)
        sel_f = jnp.where(gt[:, sl], 1.0, jnp.where(tie_rank < need, tie_f, 0.0))
        rank = _bdot(sel_f.astype(BF16), upper) + carry_sel
        offs_ref[0, :, j:j + 1] = carry_sel.astype(I32)
        carry_sel = carry_sel + jnp.sum(sel_f, axis=1, keepdims=True)
        grank_ref[0, :, sl] = jnp.where(sel_f > 0.5, rank.astype(I32), -1)
    offs_ref[0, :, s // tt:s // tt + 1] = carry_sel.astype(I32)


def _route(lg, ne, cap, tt):
    bsz, s, _ = lg.shape
    return pl.pallas_call(
        functools.partial(_route_kernel, ne=ne, cap=cap, s=s, tt=tt),
        grid=(bsz,),
        in_specs=[pl.BlockSpec((1, s, LANES), lambda b: (b, 0, 0))],
        out_specs=[pl.BlockSpec((1, ne, s), lambda b: (b, 0, 0)),
                   pl.BlockSpec((1, ne, s), lambda b: (b, 0, 0)),
                   pl.BlockSpec((1, ne, LANES), lambda b: (b, 0, 0))],
        out_shape=[jax.ShapeDtypeStruct((bsz, ne, s), I32),
                   jax.ShapeDtypeStruct((bsz, ne, s), F32),
                   jax.ShapeDtypeStruct((bsz, ne, LANES), I32)],
        compiler_params=_cparams(("arbitrary",), 40),
        name="route",
    )(lg)


def _round_plan(offs_ref, b, j, ne, nt, cap):
    base = [(b * ne + e) * (nt + 1) + j for e in range(ne)]
    off = [offs_ref[base[e]] for e in range(ne)]
    end = [offs_ref[base[e] + 1] for e in range(ne)]
    w0 = [(off[e] // BF16_ROWS) * BF16_ROWS for e in range(ne)]
    nr = jnp.int32(0)
    for e in range(ne):
        nr_e = jnp.where(end[e] > off[e], (end[e] - w0[e] + SLOT_WIN - 1) // SLOT_WIN, 0)
        nr = jnp.maximum(nr, nr_e)
    return w0, nr


def _one_hot_windows(gr, w0, r, ne, tt, cap):
    srow = lax.broadcasted_iota(I32, (SLOT_WIN, tt), 0)
    starts, masks = [], []
    for e in range(ne):
        lo = w0[e] + SLOT_WIN * r
        start = pl.multiple_of(jnp.minimum(lo, cap - SLOT_WIN), BF16_ROWS)
        g = gr[e:e + 1, :]
        g = jnp.where(g >= lo, g, -1)
        masks.append((g - start) == srow)
        starts.append(start)
    return starts, masks


def _gather_kernel(offs_ref, u2_ref, grank_ref, aff_ref, xg_ref, wrep_ref, *, ne, nt, tt, cap, nsub):
    b = pl.program_id(0)
    j = pl.program_id(1)

    @pl.when(j == 0)
    def _():
        xg_ref[...] = jnp.zeros(xg_ref.shape, BF16)
        wrep_ref[...] = jnp.zeros(wrep_ref.shape, F32)

    plans = [_round_plan(offs_ref, b, j * nsub + sub, ne, nt, cap) for sub in range(nsub)]

    def do_round(sub, r):
        cols = slice(sub * tt, (sub + 1) * tt)
        starts, masks = _one_hot_windows(grank_ref[0, :, cols], plans[sub][0], r, ne, tt, cap)
        pcat = jnp.concatenate([jnp.where(mk, 1.0, 0.0).astype(BF16) for mk in masks], axis=0)
        xw = _bdot(pcat, u2_ref[0, cols, :]).astype(BF16)
        af = aff_ref[0, :, cols]
        for e in range(ne):
            rows = pl.ds(starts[e], SLOT_WIN)
            xg_ref[0, e, rows, :] = xg_ref[0, e, rows, :] + xw[e * SLOT_WIN:(e + 1) * SLOT_WIN]
            ws = jnp.sum(jnp.where(masks[e], af[e:e + 1, :], 0.0), axis=1, keepdims=True)
            wrep_ref[0, e, rows, :] = wrep_ref[0, e, rows, :] + jnp.broadcast_to(ws, (SLOT_WIN, LANES))

    for sub in range(nsub):
        do_round(sub, 0)
    for sub in range(nsub):
        def extra(r, carry, sub=sub):
            do_round(sub, r)
            return carry
        lax.fori_loop(1, plans[sub][1], extra, 0)


def _gather(offs, u2, grank, aff, ne, cap, tt, nsub):
    bsz, s, d = u2.shape
    nt = s // tt
    tg = nsub * tt
    grid_spec = pltpu.PrefetchScalarGridSpec(
        num_scalar_prefetch=1,
        grid=(bsz, nt // nsub),
        in_specs=[pl.BlockSpec((1, tg, d), lambda b, j, o: (b, j, 0)),
                  pl.BlockSpec((1, ne, tg), lambda b, j, o: (b, 0, j)),
                  pl.BlockSpec((1, ne, tg), lambda b, j, o: (b, 0, j))],
        out_specs=[pl.BlockSpec((1, ne, cap, d), lambda b, j, o: (b, 0, 0, 0)),
                   pl.BlockSpec((1, ne, cap, LANES), lambda b, j, o: (b, 0, 0, 0))],
    )
    return pl.pallas_call(
        functools.partial(_gather_kernel, ne=ne, nt=nt, tt=tt, cap=cap, nsub=nsub),
        grid_spec=grid_spec,
        out_shape=[jax.ShapeDtypeStruct((bsz, ne, cap, d), BF16),
                   jax.ShapeDtypeStruct((bsz, ne, cap, LANES), F32)],
        compiler_params=_cparams(("arbitrary", "arbitrary"), 56),
        name="gather",
    )(offs, u2, grank, aff)


def _ffn_kernel(x_ref, wg_ref, wu_ref, wd_ref, wrep_ref, o_ref, acc_ref, *, nb, cap, d):
    f = pl.program_id(2)
    last = pl.num_programs(2) - 1
    wg = wg_ref[0].astype(BF16)
    wu = wu_ref[0].astype(BF16)
    wd = wd_ref[0].astype(BF16)

    def partial_out(i):
        x = x_ref[i, 0]
        hg = _bdot(x, wg)
        hu = _bdot(x, wu)
        act = (hg * jax.nn.sigmoid(hg) * hu).astype(BF16)
        return _bdot(act, wd)

    @pl.when(f == 0)
    def _():
        for i in range(nb):
            acc_ref[i] = partial_out(i)

    @pl.when(jnp.logical_and(f > 0, f < last))
    def _():
        for i in range(nb):
            acc_ref[i] += partial_out(i)

    @pl.when(f == last)
    def _():
        for i in range(nb):
            tot = acc_ref[i] + partial_out(i)
            w = wrep_ref[i, 0]
            for c in range(d // LANES):
                cols = slice(c * LANES, (c + 1) * LANES)
                o_ref[i, 0, :, cols] = (tot[:, cols] * w).astype(BF16)


def _ffn(xg, wrep, w_gate, w_up, w_down, nb, tf):
    bsz, ne, cap, d = xg.shape
    ff = w_gate.shape[2]
    assert ff % tf == 0 and ff // tf >= 2
    return pl.pallas_call(
        functools.partial(_ffn_kernel, nb=nb, cap=cap, d=d),
        grid=(ne, bsz // nb, ff // tf),
        in_specs=[pl.BlockSpec((nb, 1, cap, d), lambda e, m, f: (m, e, 0, 0)),
                  pl.BlockSpec((1, d, tf), lambda e, m, f: (e, 0, f)),
                  pl.BlockSpec((1, d, tf), lambda e, m, f: (e, 0, f)),
                  pl.BlockSpec((1, tf, d), lambda e, m, f: (e, f, 0)),
                  pl.BlockSpec((nb, 1, cap, LANES), lambda e, m, f: (m, e, 0, 0))],
        out_specs=pl.BlockSpec((nb, 1, cap, d), lambda e, m, f: (m, e, 0, 0)),
        out_shape=jax.ShapeDtypeStruct((bsz, ne, cap, d), BF16),
        scratch_shapes=[pltpu.VMEM((nb, cap, d), F32)],
        compiler_params=_cparams(("arbitrary", "arbitrary", "arbitrary"), 56),
        name="ffn",
    )(xg, w_gate, w_up, w_down, wrep)


def _combine_kernel(offs_ref, grank_ref, eo_ref, h_ref, mod_ref, gf_ref, o_ref, y_ref, *,
                    ne, nt, tt, cap, nsub, final):
    b = pl.program_id(0)
    j = pl.program_id(1)
    plans = [_round_plan(offs_ref, b, j * nsub + sub, ne, nt, cap) for sub in range(nsub)]

    def round_sum(sub, r):
        cols = slice(sub * tt, (sub + 1) * tt)
        starts, masks = _one_hot_windows(grank_ref[0, :, cols], plans[sub][0], r, ne, tt, cap)
        pcat = jnp.concatenate([jnp.where(mk, 1.0, 0.0).astype(BF16) for mk in masks], axis=0)
        ocat = jnp.concatenate([eo_ref[0, e, pl.ds(starts[e], SLOT_WIN), :] for e in range(ne)], axis=0)
        return lax.dot_general(pcat, ocat, (((0,), (0,)), ((), ())), preferred_element_type=F32)

    for sub in range(nsub):
        y_ref[sub * tt:(sub + 1) * tt, :] = round_sum(sub, 0)
    for sub in range(nsub):
        def extra(r, carry, sub=sub):
            y_ref[sub * tt:(sub + 1) * tt, :] += round_sum(sub, r)
            return carry
        lax.fori_loop(1, plans[sub][1], extra, 0)
    hout = h_ref[0] + mod_ref[0, 5:6, :] * y_ref[...]
    if final:
        hout = hout * lax.rsqrt(jnp.mean(hout * hout, axis=-1, keepdims=True) + RMS_EPS) * gf_ref[...]
    o_ref[0] = hout


def _combine(offs, grank, eo, h, mod, gf, tt, nsub, final):
    bsz, s, d = h.shape
    ne, cap = eo.shape[1], eo.shape[2]
    nt = s // tt
    tg = nsub * tt
    grid_spec = pltpu.PrefetchScalarGridSpec(
        num_scalar_prefetch=1,
        grid=(bsz, nt // nsub),
        in_specs=[pl.BlockSpec((1, ne, tg), lambda b, j, o: (b, 0, j)),
                  pl.BlockSpec((1, ne, cap, d), lambda b, j, o: (b, 0, 0, 0)),
                  pl.BlockSpec((1, tg, d), lambda b, j, o: (b, j, 0)),
                  pl.BlockSpec((1, N_MOD, d), lambda b, j, o: (b, 0, 0)),
                  pl.BlockSpec((1, d), lambda b, j, o: (0, 0))],
        out_specs=pl.BlockSpec((1, tg, d), lambda b, j, o: (b, j, 0)),
        scratch_shapes=[pltpu.VMEM((tg, d), F32)],
    )
    return pl.pallas_call(
        functools.partial(_combine_kernel, ne=ne, nt=nt, tt=tt, cap=cap, nsub=nsub, final=final),
        grid_spec=grid_spec,
        out_shape=jax.ShapeDtypeStruct((bsz, s, d), F32),
        compiler_params=_cparams(("arbitrary", "arbitrary"), 56),
        name="combine",
    )(offs, grank, eo, h, mod, gf)


def kernel(x, c, ada_w, ada_b, norm1_g, w_in, b_in, conv_dw_w, conv_dw_b, conv_ln_g, conv_ln_b,
           conv_w_out, conv_b_out, fourier_w, fourier_b, w_out, b_out, norm2_g, router_w,
           expert_w_gate, expert_w_up, expert_w_down, final_norm_g):
    bsz, s, d = x.shape
    depth = ada_w.shape[0]
    cw = conv_dw_w.shape[2]
    fw = fourier_w.shape[1]
    ne = router_w.shape[2]
    cap = CAPACITY_FACTOR * s // ne
    tt = min(ROUTE_TILE, s)
    ts = min(512, s)
    assert s % DFT_RADIX == 0 and s % tt == 0 and cap % BF16_ROWS == 0 and cap >= SLOT_WIN and ne <= LANES
    wg_np, tab_np = _dft_constants(s, fw)
    dft_wg = jnp.asarray(wg_np).astype(BF16)
    dft_tab = jnp.asarray(tab_np).astype(BF16)
    row = lambda a: a.reshape(1, -1)

    h = x
    for l in range(depth):
        mod = _ada(c, ada_w[l], ada_b[l]).reshape(bsz, N_MOD, d)
        c1 = 2 * cw + fw
        v, f = _proj_cf(h, mod, row(norm1_g[l]), w_in[l][:, :c1].astype(BF16), row(b_in[l][:c1]), cw, fw, ts)
        fr = _fourier(f, dft_wg, dft_tab)
        rw = jnp.zeros((d, LANES), BF16).at[:, :ne].set(router_w[l].astype(BF16))
        h, u2, lg = _mixer(h, mod, row(norm1_g[l]), w_in[l][:, c1:].astype(BF16), row(b_in[l][c1:]), v,
                           conv_dw_w[l], row(conv_dw_b[l]), row(conv_ln_g[l]), row(conv_ln_b[l]),
                           conv_w_out[l].astype(BF16), row(conv_b_out[l]), fr,
                           fourier_w[l].astype(BF16), row(fourier_b[l]),
                           w_out[l].astype(BF16), row(b_out[l]), row(norm2_g[l]), rw, ts)
        grank, aff, offs = _route(lg, ne, cap, tt)
        offs_flat = offs[:, :, :s // tt + 1].reshape(-1)
        nsub = 2 if (s // tt) % 2 == 0 else 1
        xg, wrep = _gather(offs_flat, u2, grank, aff, ne, cap, tt, nsub)
        nb = 4 if bsz % 4 == 0 else 1
        eo = _ffn(xg, wrep, expert_w_gate[l], expert_w_up[l], expert_w_down[l], nb, min(512, expert_w_gate.shape[3] // 2))
        h = _combine(offs_flat, grank, eo, h, mod, row(final_norm_g), tt, nsub, final=(l == depth - 1))
    return h
```

```python
import functools

import numpy as np
import jax
import jax.numpy as jnp
from jax import lax
from jax.experimental import pallas as pl
from jax.experimental.pallas import tpu as pltpu

F32 = jnp.float32
BF16 = jnp.bfloat16
I32 = jnp.int32

RMS_EPS = 1e-6
LN_EPS = 1e-5
FOURIER_GROUPS = 4
CAPACITY_FACTOR = 2
N_MOD = 6

LANES = 128
SUBLANES = 8
BF16_ROWS = 16
DFT_RADIX = 4
ROUTE_TILE = 256
SLOT_WIN = 64
MIB = 1024 * 1024


def _cparams(sem, vmem_mib):
    return pltpu.CompilerParams(dimension_semantics=sem, vmem_limit_bytes=vmem_mib * MIB)


def _const_spec(shape):
    nd = len(shape)
    return pl.BlockSpec(shape, lambda *_: (0,) * nd)


def _rms_mod(x, g, shift, scale):
    return x * lax.rsqrt(jnp.mean(x * x, axis=-1, keepdims=True) + RMS_EPS) * (g * (1.0 + scale)) + shift


def _bdot(a, b):
    return jnp.dot(a, b, preferred_element_type=F32)


def _ada_kernel(c_ref, w_ref, b_ref, o_ref):
    c = c_ref[...]
    ca = c * jax.nn.sigmoid(c)
    o_ref[...] = _bdot(ca.astype(BF16), w_ref[...].astype(BF16)) + b_ref[...]


def _ada(c, w, b):
    bsz, d = c.shape
    n = w.shape[1]
    tn = min(n, 1536)
    return pl.pallas_call(
        _ada_kernel,
        grid=(n // tn,),
        in_specs=[_const_spec((bsz, d)),
                  pl.BlockSpec((d, tn), lambda j: (0, j)),
                  pl.BlockSpec((1, tn), lambda j: (0, j))],
        out_specs=pl.BlockSpec((bsz, tn), lambda j: (0, j)),
        out_shape=jax.ShapeDtypeStruct((bsz, n), F32),
        compiler_params=_cparams(("arbitrary",), 40),
        name="ada",
    )(c, w, b.reshape(1, n))


def _proj_cf_kernel(x_ref, mod_ref, g_ref, w_ref, b_ref, v_ref, f_ref, *, cw):
    u = _rms_mod(x_ref[0], g_ref[...], mod_ref[0, 0:1, :], mod_ref[0, 1:2, :])
    p = _bdot(u.astype(BF16), w_ref[...]) + b_ref[...]
    a = p[:, :cw]
    g = p[:, cw:2 * cw]
    v_ref[0] = (a * jax.nn.sigmoid(g)).astype(BF16)
    f_ref[0] = p[:, 2 * cw:].astype(BF16)


def _proj_cf(x, mod, g1, w_cf, b_cf, cw, fw, ts):
    bsz, s, d = x.shape
    n = w_cf.shape[1]
    tile = lambda m: pl.BlockSpec((1, ts, m), lambda b, j: (b, j, 0))
    return pl.pallas_call(
        functools.partial(_proj_cf_kernel, cw=cw),
        grid=(bsz, s // ts),
        in_specs=[tile(d),
                  pl.BlockSpec((1, N_MOD, d), lambda b, j: (b, 0, 0)),
                  _const_spec((1, d)),
                  _const_spec((d, n)),
                  _const_spec((1, n))],
        out_specs=[tile(cw), tile(fw)],
        out_shape=[jax.ShapeDtypeStruct((bsz, s, cw), BF16),
                   jax.ShapeDtypeStruct((bsz, s, fw), BF16)],
        compiler_params=_cparams(("arbitrary", "arbitrary"), 40),
        name="proj_cf",
    )(x, mod, g1, w_cf, b_cf)


def _group_block(fw):
    gd = fw // FOURIER_GROUPS
    return gd * max(1, min(FOURIER_GROUPS, LANES // gd))


def _dft_constants(s, fw):
    gd = fw // FOURIER_GROUPS
    q = s // DFT_RADIX
    j = np.arange(gd)
    ang = 2.0 * np.pi * np.outer(j, j) / gd
    scale = 1.0 / np.sqrt(float(s) * gd)
    wb = _group_block(fw)
    wr = np.zeros((wb, wb), np.float64)
    wi = np.zeros((wb, wb), np.float64)
    for g in range(wb // gd):
        sl = slice(g * gd, (g + 1) * gd)
        wr[sl, sl] = np.cos(ang) * scale
        wi[sl, sl] = -np.sin(ang) * scale
    wg = np.concatenate([wr, wi], axis=1).astype(np.float32)
    b = np.arange(q, dtype=np.int64)
    k = (DFT_RADIX * np.arange(q, dtype=np.int64)[None, :, None]
         + np.arange(DFT_RADIX, dtype=np.int64)[:, None, None])
    m = (k * b[None, None, :]) % s
    ang2 = 2.0 * np.pi * m.astype(np.float64) / s
    tab = np.concatenate([np.cos(ang2), np.sin(ang2)], axis=2).astype(np.float32)
    return wg, tab


def _fourier_kernel(f_ref, wg_ref, tab_ref, o_ref, a_ref, y_ref, *, q, fw, rc, pk):
    wg = wg_ref[...]
    wb = wg.shape[0]
    for c in range(q // rc):
        zr, zi = [], []
        for a in range(DFT_RADIX):
            fa = f_ref[0, a * q + c * rc:a * q + (c + 1) * rc, :]
            z = [_bdot(fa[:, g * wb:(g + 1) * wb], wg) for g in range(fw // wb)]
            zr.append(jnp.concatenate([t[:, :wb] for t in z], axis=1))
            zi.append(jnp.concatenate([t[:, wb:] for t in z], axis=1))
        er, ei = zr[0] + zr[2], zi[0] + zi[2]
        orr, oi = zr[1] + zr[3], zi[1] + zi[3]
        dr, di = zr[0] - zr[2], zi[0] - zi[2]
        pr, pi_ = zr[1] - zr[3], zi[1] - zi[3]
        ar = [er + orr, dr + pi_, er - orr, dr - pi_]
        ai = [ei + oi, di - pr, ei - oi, di + pr]
        for k1 in range(DFT_RADIX):
            a_ref[k1, c * rc:(c + 1) * rc, :] = ar[k1].astype(BF16)
            a_ref[k1, q + c * rc:q + (c + 1) * rc, :] = ai[k1].astype(BF16)
    for k1 in range(DFT_RADIX):
        y_ref[k1] = _bdot(tab_ref[k1], a_ref[k1]).astype(BF16)
    pt = DFT_RADIX * pk
    r_i = lax.broadcasted_iota(I32, (pt, pt), 0)
    c_i = lax.broadcasted_iota(I32, (pt, pt), 1)
    src = DFT_RADIX * jnp.bitwise_and(c_i, pk - 1) + jnp.right_shift(c_i, pk.bit_length() - 1)
    perm = jnp.where(r_i == src, 1.0, 0.0).astype(BF16)
    for t in range(q // pk):
        yc = jnp.concatenate([y_ref[k1, t * pk:(t + 1) * pk, :] for k1 in range(DFT_RADIX)], axis=0)
        o_ref[0, t * pt:(t + 1) * pt, :] = _bdot(perm, yc).astype(BF16)


def _fourier(f, wg, tab):
    bsz, s, fw = f.shape
    q = s // DFT_RADIX
    rc = min(q, 256)
    pk = min(q, LANES)
    assert pk & (pk - 1) == 0 and q % pk == 0
    return pl.pallas_call(
        functools.partial(_fourier_kernel, q=q, fw=fw, rc=rc, pk=pk),
        grid=(bsz,),
        in_specs=[pl.BlockSpec((1, s, fw), lambda b: (b, 0, 0)),
                  pl.BlockSpec(wg.shape, lambda b: (0, 0), pipeline_mode=pl.Buffered(1)),
                  pl.BlockSpec((DFT_RADIX, q, 2 * q), lambda b: (0, 0, 0), pipeline_mode=pl.Buffered(1))],
        out_specs=pl.BlockSpec((1, s, fw), lambda b: (b, 0, 0)),
        out_shape=jax.ShapeDtypeStruct((bsz, s, fw), BF16),
        scratch_shapes=[pltpu.VMEM((DFT_RADIX, 2 * q, fw), BF16),
                        pltpu.VMEM((DFT_RADIX, q, fw), BF16)],
        compiler_params=_cparams(("arbitrary",), 56),
        name="fourier",
    )(f, wg, tab)


def _mixer_kernel(x_ref, mod_ref, g1_ref, wg_ref, bg_ref, v_ref, dww_ref, dwb_ref, lng_ref, lnb_ref,
                  wco_ref, bco_ref, fr_ref, wf_ref, bf_ref, wo_ref, bo_ref, g2_ref, rw_ref,
                  h_ref, u2_ref, lg_ref, vwin_ref, *, ts, tsb, s, d, kw, halo):
    j = pl.program_id(1)
    nj = pl.num_programs(1)
    s0 = pl.multiple_of(j * ts, ts)
    pad = kw // 2
    mod = mod_ref[0]

    vwin_ref[halo:halo + ts, :] = v_ref[0, pl.ds(s0, ts), :].astype(F32)
    lo = pl.multiple_of(jnp.maximum(s0 - halo, 0), halo)
    vwin_ref[0:halo, :] = jnp.where(j > 0, v_ref[0, pl.ds(lo, halo), :].astype(F32), 0.0)
    hi = pl.multiple_of(jnp.minimum(s0 + ts, s - halo), halo)
    vwin_ref[halo + ts:halo + ts + halo, :] = jnp.where(
        j < nj - 1, v_ref[0, pl.ds(hi, halo), :].astype(F32), 0.0)

    first = halo - pad
    for sb in range(ts // tsb):
        r0 = sb * tsb
        rows = slice(r0, r0 + tsb)
        x = x_ref[0, rows, :]
        u = _rms_mod(x, g1_ref[...], mod[0:1], mod[1:2])
        gates = jax.nn.sigmoid(_bdot(u.astype(BF16), wg_ref[...]) + bg_ref[...])
        acc = None
        for r in range(SUBLANES):
            part = None
            for qq in range((first + kw + SUBLANES - 1) // SUBLANES):
                o = SUBLANES * qq + r
                if first <= o < first + kw:
                    win = vwin_ref[r0 + SUBLANES * qq:r0 + SUBLANES * qq + tsb + SUBLANES, :]
                    term = win * dww_ref[o - first:o - first + 1, :]
                    part = term if part is None else part + term
            if part is None:
                continue
            shifted = part[:tsb] if r == 0 else pltpu.roll(part, tsb + SUBLANES - r, axis=0)[:tsb]
            acc = shifted + dwb_ref[...] if acc is None else acc + shifted
        mu = jnp.mean(acc, axis=-1, keepdims=True)
        xc = acc - mu
        var = jnp.mean(xc * xc, axis=-1, keepdims=True)
        cv = xc * lax.rsqrt(var + LN_EPS) * lng_ref[...] + lnb_ref[...]
        cv = cv * jax.nn.sigmoid(cv)
        y_conv = _bdot(cv.astype(BF16), wco_ref[...]) + bco_ref[...]
        y_four = _bdot(fr_ref[0, rows, :], wf_ref[...]) + bf_ref[...]
        merged = gates[:, :d] * y_conv + gates[:, d:] * y_four
        o_proj = _bdot(merged.astype(BF16), wo_ref[...]) + bo_ref[...]
        h = x + mod[2:3] * o_proj
        h_ref[0, rows, :] = h
        u2b = _rms_mod(h, g2_ref[...], mod[3:4], mod[4:5]).astype(BF16)
        u2_ref[0, rows, :] = u2b
        lg_ref[0, rows, :] = _bdot(u2b, rw_ref[...])


def _mixer(x, mod, g1, w_g, b_g, v, dww, dwb, lng, lnb, wco, bco, fr, wf, bf, wo, bo, g2, rw, ts):
    bsz, s, d = x.shape
    cw = v.shape[2]
    fw = fr.shape[2]
    kw = dww.shape[0]
    halo = BF16_ROWS
    tsb = ts
    assert kw // 2 <= halo and s % ts == 0 and ts % tsb == 0 and tsb % halo == 0
    tile = lambda n: pl.BlockSpec((1, ts, n), lambda b, j: (b, j, 0))
    return pl.pallas_call(
        functools.partial(_mixer_kernel, ts=ts, tsb=tsb, s=s, d=d, kw=kw, halo=halo),
        grid=(bsz, s // ts),
        in_specs=[tile(d),
                  pl.BlockSpec((1, N_MOD, d), lambda b, j: (b, 0, 0)),
                  _const_spec((1, d)),
                  _const_spec((d, 2 * d)), _const_spec((1, 2 * d)),
                  pl.BlockSpec((1, s, cw), lambda b, j: (b, 0, 0)),
                  _const_spec((kw, cw)), _const_spec((1, cw)), _const_spec((1, cw)), _const_spec((1, cw)),
                  _const_spec((cw, d)), _const_spec((1, d)),
                  tile(fw),
                  _const_spec((fw, d)), _const_spec((1, d)),
                  _const_spec((d, d)), _const_spec((1, d)),
                  _const_spec((1, d)),
                  _const_spec((d, LANES))],
        out_specs=[tile(d), tile(d), tile(LANES)],
        out_shape=[jax.ShapeDtypeStruct((bsz, s, d), F32),
                   jax.ShapeDtypeStruct((bsz, s, d), BF16),
                   jax.ShapeDtypeStruct((bsz, s, LANES), F32)],
        scratch_shapes=[pltpu.VMEM((ts + 2 * halo, cw), F32)],
        compiler_params=_cparams(("arbitrary", "arbitrary"), 56),
        name="mixer",
    )(x, mod, g1, w_g, b_g, v, dww, dwb, lng, lnb, wco, bco, fr, wf, bf, wo, bo, g2, rw)


def _route_kernel(lg_ref, grank_ref, aff_ref, offs_ref, *, ne, cap, s, tt):
    lt = lg_ref[0].T[:ne, :]
    m = jnp.max(lt, axis=0, keepdims=True)
    e = jnp.exp(lt - m)
    aff = e / jnp.sum(e, axis=0, keepdims=True)
    aff_ref[0] = aff

    def enough(cand_bits):
        cnt = jnp.sum(jnp.where(aff >= pltpu.bitcast(cand_bits, F32), 1.0, 0.0), axis=1, keepdims=True)
        return cnt >= cap

    def search(i, thr_bits):
        hi = jnp.left_shift(jnp.int32(1), 29 - 2 * i)
        lo = jnp.left_shift(jnp.int32(1), 28 - 2 * i)
        c_hi, c_lo, c_both = thr_bits | hi, thr_bits | lo, thr_bits | hi | lo
        return jnp.where(enough(c_both), c_both,
                         jnp.where(enough(c_hi), c_hi, jnp.where(enough(c_lo), c_lo, thr_bits)))

    thr = pltpu.bitcast(lax.fori_loop(0, 15, search, jnp.zeros((ne, 1), I32)), F32)
    gt = aff > thr
    tie = aff == thr
    need = cap - jnp.sum(jnp.where(gt, 1.0, 0.0), axis=1, keepdims=True)

    r_i = lax.broadcasted_iota(I32, (tt, tt), 0)
    c_i = lax.broadcasted_iota(I32, (tt, tt), 1)
    upper = jnp.where(r_i < c_i, 1.0, 0.0).astype(BF16)
    offs_ref[0] = jnp.zeros(offs_ref.shape[1:], I32)
    carry_tie = jnp.zeros((ne, 1), F32)
    carry_sel = jnp.zeros((ne, 1), F32)
    for j in range(s // tt):
        sl = slice(j * tt, (j + 1) * tt)
        tie_f = jnp.where(tie[:, sl], 1.0, 0.0)
        tie_rank = _bdot(tie_f.astype(BF16), upper) + carry_tie
        carry_tie = carry_tie + jnp.sum(tie_f, axis=1, keepdims=True)
        sel_f = jnp.where(gt[:, sl], 1.0, jnp.where(tie_rank < need, tie_f, 0.0))
        rank = _bdot(sel_f.astype(BF16), upper) + carry_sel
        offs_ref[0, :, j:j + 1] = carry_sel.astype(I32)
        carry_sel = carry_sel + jnp.sum(sel_f, axis=1, keepdims=True)
        grank_ref[0, :, sl] = jnp.where(sel_f > 0.5, rank.astype(I32), -1)
    offs_ref[0, :, s // tt:s // tt + 1] = carry_sel.astype(I32)


def _route(lg, ne, cap, tt):
    bsz, s, _ = lg.shape
    return pl.pallas_call(
        functools.partial(_route_kernel, ne=ne, cap=cap, s=s, tt=tt),
        grid=(bsz,),
        in_specs=[pl.BlockSpec((1, s, LANES), lambda b: (b, 0, 0))],
        out_specs=[pl.BlockSpec((1, ne, s), lambda b: (b, 0, 0)),
                   pl.BlockSpec((1, ne, s), lambda b: (b, 0, 0)),
                   pl.BlockSpec((1, ne, LANES), lambda b: (b, 0, 0))],
        out_shape=[jax.ShapeDtypeStruct((bsz, ne, s), I32),
                   jax.ShapeDtypeStruct((bsz, ne, s), F32),
                   jax.ShapeDtypeStruct((bsz, ne, LANES), I32)],
        compiler_params=_cparams(("arbitrary",), 40),
        name="route",
    )(lg)


def _round_plan(offs_ref, b, j, ne, nt, cap):
    base = [(b * ne + e) * (nt + 1) + j for e in range(ne)]
    off = [offs_ref[base[e]] for e in range(ne)]
    end = [offs_ref[base[e] + 1] for e in range(ne)]
    w0 = [(off[e] // BF16_ROWS) * BF16_ROWS for e in range(ne)]
    nr = jnp.int32(0)
    for e in range(ne):
        nr_e = jnp.where(end[e] > off[e], (end[e] - w0[e] + SLOT_WIN - 1) // SLOT_WIN, 0)
        nr = jnp.maximum(nr, nr_e)
    return w0, nr


def _one_hot_windows(gr, w0, r, ne, tt, cap):
    srow = lax.broadcasted_iota(I32, (SLOT_WIN, tt), 0)
    starts, masks = [], []
    for e in range(ne):
        lo = w0[e] + SLOT_WIN * r
        start = pl.multiple_of(jnp.minimum(lo, cap - SLOT_WIN), BF16_ROWS)
        g = gr[e:e + 1, :]
        g = jnp.where(g >= lo, g, -1)
        masks.append((g - start) == srow)
        starts.append(start)
    return starts, masks


def _gather_kernel(offs_ref, u2_ref, grank_ref, aff_ref, xg_ref, wrep_ref, *, ne, nt, tt, cap, nsub):
    b = pl.program_id(0)
    j = pl.program_id(1)

    @pl.when(j == 0)
    def _():
        xg_ref[...] = jnp.zeros(xg_ref.shape, BF16)
        wrep_ref[...] = jnp.zeros(wrep_ref.shape, F32)

    plans = [_round_plan(offs_ref, b, j * nsub + sub, ne, nt, cap) for sub in range(nsub)]

    def do_round(sub, r):
        cols = slice(sub * tt, (sub + 1) * tt)
        starts, masks = _one_hot_windows(grank_ref[0, :, cols], plans[sub][0], r, ne, tt, cap)
        pcat = jnp.concatenate([jnp.where(mk, 1.0, 0.0).astype(BF16) for mk in masks], axis=0)
        xw = _bdot(pcat, u2_ref[0, cols, :]).astype(BF16)
        af = aff_ref[0, :, cols]
        for e in range(ne):
            rows = pl.ds(starts[e], SLOT_WIN)
            xg_ref[0, e, rows, :] = xg_ref[0, e, rows, :] + xw[e * SLOT_WIN:(e + 1) * SLOT_WIN]
            ws = jnp.sum(jnp.where(masks[e], af[e:e + 1, :], 0.0), axis=1, keepdims=True)
            wrep_ref[0, e, rows, :] = wrep_ref[0, e, rows, :] + jnp.broadcast_to(ws, (SLOT_WIN, LANES))

    for sub in range(nsub):
        do_round(sub, 0)
    for sub in range(nsub):
        def extra(r, carry, sub=sub):
            do_round(sub, r)
            return carry
        lax.fori_loop(1, plans[sub][1], extra, 0)


def _gather(offs, u2, grank, aff, ne, cap, tt, nsub):
    bsz, s, d = u2.shape
    nt = s // tt
    tg = nsub * tt
    grid_spec = pltpu.PrefetchScalarGridSpec(
        num_scalar_prefetch=1,
        grid=(bsz, nt // nsub),
        in_specs=[pl.BlockSpec((1, tg, d), lambda b, j, o: (b, j, 0)),
                  pl.BlockSpec((1, ne, tg), lambda b, j, o: (b, 0, j)),
                  pl.BlockSpec((1, ne, tg), lambda b, j, o: (b, 0, j))],
        out_specs=[pl.BlockSpec((1, ne, cap, d), lambda b, j, o: (b, 0, 0, 0)),
                   pl.BlockSpec((1, ne, cap, LANES), lambda b, j, o: (b, 0, 0, 0))],
    )
    return pl.pallas_call(
        functools.partial(_gather_kernel, ne=ne, nt=nt, tt=tt, cap=cap, nsub=nsub),
        grid_spec=grid_spec,
        out_shape=[jax.ShapeDtypeStruct((bsz, ne, cap, d), BF16),
                   jax.ShapeDtypeStruct((bsz, ne, cap, LANES), F32)],
        compiler_params=_cparams(("arbitrary", "arbitrary"), 56),
        name="gather",
    )(offs, u2, grank, aff)


def _ffn_kernel(x_ref, wg_ref, wu_ref, wd_ref, wrep_ref, o_ref, acc_ref, *, nb, cap, d):
    f = pl.program_id(2)
    last = pl.num_programs(2) - 1
    wg = wg_ref[0].astype(BF16)
    wu = wu_ref[0].astype(BF16)
    wd = wd_ref[0].astype(BF16)

    def partial_out(i):
        x = x_ref[i, 0]
        hg = _bdot(x, wg)
        hu = _bdot(x, wu)
        act = (hg * jax.nn.sigmoid(hg) * hu).astype(BF16)
        return _bdot(act, wd)

    @pl.when(f == 0)
    def _():
        for i in range(nb):
            acc_ref[i] = partial_out(i)

    @pl.when(jnp.logical_and(f > 0, f < last))
    def _():
        for i in range(nb):
            acc_ref[i] += partial_out(i)

    @pl.when(f == last)
    def _():
        for i in range(nb):
            tot = acc_ref[i] + partial_out(i)
            w = wrep_ref[i, 0]
            for c in range(d // LANES):
                cols = slice(c * LANES, (c + 1) * LANES)
                o_ref[i, 0, :, cols] = (tot[:, cols] * w).astype(BF16)


def _ffn(xg, wrep, w_gate, w_up, w_down, nb, tf):
    bsz, ne, cap, d = xg.shape
    ff = w_gate.shape[2]
    assert ff % tf == 0 and ff // tf >= 2
    return pl.pallas_call(
        functools.partial(_ffn_kernel, nb=nb, cap=cap, d=d),
        grid=(ne, bsz // nb, ff // tf),
        in_specs=[pl.BlockSpec((nb, 1, cap, d), lambda e, m, f: (m, e, 0, 0)),
                  pl.BlockSpec((1, d, tf), lambda e, m, f: (e, 0, f)),
                  pl.BlockSpec((1, d, tf), lambda e, m, f: (e, 0, f)),
                  pl.BlockSpec((1, tf, d), lambda e, m, f: (e, f, 0)),
                  pl.BlockSpec((nb, 1, cap, LANES), lambda e, m, f: (m, e, 0, 0))],
        out_specs=pl.BlockSpec((nb, 1, cap, d), lambda e, m, f: (m, e, 0, 0)),
        out_shape=jax.ShapeDtypeStruct((bsz, ne, cap, d), BF16),
        scratch_shapes=[pltpu.VMEM((nb, cap, d), F32)],
        compiler_params=_cparams(("arbitrary", "arbitrary", "arbitrary"), 56),
        name="ffn",
    )(xg, w_gate, w_up, w_down, wrep)


def _combine_kernel(offs_ref, grank_ref, eo_ref, h_ref, mod_ref, gf_ref, o_ref, y_ref, *,
                    ne, nt, tt, cap, nsub, final):
    b = pl.program_id(0)
    j = pl.program_id(1)
    plans = [_round_plan(offs_ref, b, j * nsub + sub, ne, nt, cap) for sub in range(nsub)]

    def round_sum(sub, r):
        cols = slice(sub * tt, (sub + 1) * tt)
        starts, masks = _one_hot_windows(grank_ref[0, :, cols], plans[sub][0], r, ne, tt, cap)
        pcat = jnp.concatenate([jnp.where(mk, 1.0, 0.0).astype(BF16) for mk in masks], axis=0)
        ocat = jnp.concatenate([eo_ref[0, e, pl.ds(starts[e], SLOT_WIN), :] for e in range(ne)], axis=0)
        return lax.dot_general(pcat, ocat, (((0,), (0,)), ((), ())), preferred_element_type=F32)

    for sub in range(nsub):
        y_ref[sub * tt:(sub + 1) * tt, :] = round_sum(sub, 0)
    for sub in range(nsub):
        def extra(r, carry, sub=sub):
            y_ref[sub * tt:(sub + 1) * tt, :] += round_sum(sub, r)
            return carry
        lax.fori_loop(1, plans[sub][1], extra, 0)
    hout = h_ref[0] + mod_ref[0, 5:6, :] * y_ref[...]
    if final:
        hout = hout * lax.rsqrt(jnp.mean(hout * hout, axis=-1, keepdims=True) + RMS_EPS) * gf_ref[...]
    o_ref[0] = hout


def _combine(offs, grank, eo, h, mod, gf, tt, nsub, final):
    bsz, s, d = h.shape
    ne, cap = eo.shape[1], eo.shape[2]
    nt = s // tt
    tg = nsub * tt
    grid_spec = pltpu.PrefetchScalarGridSpec(
        num_scalar_prefetch=1,
        grid=(bsz, nt // nsub),
        in_specs=[pl.BlockSpec((1, ne, tg), lambda b, j, o: (b, 0, j)),
                  pl.BlockSpec((1, ne, cap, d), lambda b, j, o: (b, 0, 0, 0)),
                  pl.BlockSpec((1, tg, d), lambda b, j, o: (b, j, 0)),
                  pl.BlockSpec((1, N_MOD, d), lambda b, j, o: (b, 0, 0)),
                  pl.BlockSpec((1, d), lambda b, j, o: (0, 0))],
        out_specs=pl.BlockSpec((1, tg, d), lambda b, j, o: (b, j, 0)),
        scratch_shapes=[pltpu.VMEM((tg, d), F32)],
    )
    return pl.pallas_call(
        functools.partial(_combine_kernel, ne=ne, nt=nt, tt=tt, cap=cap, nsub=nsub, final=final),
        grid_spec=grid_spec,
        out_shape=jax.ShapeDtypeStruct((bsz, s, d), F32),
        compiler_params=_cparams(("arbitrary", "arbitrary"), 56),
        name="combine",
    )(offs, grank, eo, h, mod, gf)


def kernel(x, c, ada_w, ada_b, norm1_g, w_in, b_in, conv_dw_w, conv_dw_b, conv_ln_g, conv_ln_b,
           conv_w_out, conv_b_out, fourier_w, fourier_b, w_out, b_out, norm2_g, router_w,
           expert_w_gate, expert_w_up, expert_w_down, final_norm_g):
    bsz, s, d = x.shape
    depth = ada_w.shape[0]
    cw = conv_dw_w.shape[2]
    fw = fourier_w.shape[1]
    ne = router_w.shape[2]
    cap = CAPACITY_FACTOR * s // ne
    tt = min(ROUTE_TILE, s)
    ts = min(512, s)
    assert s % DFT_RADIX == 0 and s % tt == 0 and cap % BF16_ROWS == 0 and cap >= SLOT_WIN and ne <= LANES
    wg_np, tab_np = _dft_constants(s, fw)
    dft_wg = jnp.asarray(wg_np).astype(BF16)
    dft_tab = jnp.asarray(tab_np).astype(BF16)
    row = lambda a: a.reshape(1, -1)

    h = x
    for l in range(depth):
        mod = _ada(c, ada_w[l], ada_b[l]).reshape(bsz, N_MOD, d)
        c1 = 2 * cw + fw
        v, f = _proj_cf(h, mod, row(norm1_g[l]), w_in[l][:, :c1].astype(BF16), row(b_in[l][:c1]), cw, fw, ts)
        fr = _fourier(f, dft_wg, dft_tab)
        rw = jnp.zeros((d, LANES), BF16).at[:, :ne].set(router_w[l].astype(BF16))
        h, u2, lg = _mixer(h, mod, row(norm1_g[l]), w_in[l][:, c1:].astype(BF16), row(b_in[l][c1:]), v,
                           conv_dw_w[l], row(conv_dw_b[l]), row(conv_ln_g[l]), row(conv_ln_b[l]),
                           conv_w_out[l].astype(BF16), row(conv_b_out[l]), fr,
                           fourier_w[l].astype(BF16), row(fourier_b[l]),
                           w_out[l].astype(BF16), row(b_out[l]), row(norm2_g[l]), rw, ts)
        grank, aff, offs = _route(lg, ne, cap, tt)
        offs_flat = offs[:, :, :s // tt + 1].reshape(-1)
        nsub = 2 if (s // tt) % 2 == 0 else 1
        xg, wrep = _gather(offs_flat, u2, grank, aff, ne, cap, tt, nsub)
        nb = 4 if bsz % 4 == 0 else 1
        eo = _ffn(xg, wrep, expert_w_gate[l], expert_w_up[l], expert_w_down[l], nb, min(512, expert_w_gate.shape[3] // 2))
        h = _combine(offs_flat, grank, eo, h, mod, row(final_norm_g), tt, nsub, final=(l == depth - 1))
    return h
```

```python
import functools

import numpy as np
import jax
import jax.numpy as jnp
from jax import lax
from jax.experimental import pallas as pl
from jax.experimental.pallas import tpu as pltpu

F32 = jnp.float32
BF16 = jnp.bfloat16
I32 = jnp.int32

RMS_EPS = 1e-6
LN_EPS = 1e-5
FOURIER_GROUPS = 4
CAPACITY_FACTOR = 2
N_MOD = 6

LANES = 128
SUBLANES = 8
BF16_ROWS = 16
DFT_RADIX = 4
ROUTE_TILE = 256
SLOT_WIN = 64
MIB = 1024 * 1024


def _cparams(sem, vmem_mib):
    return pltpu.CompilerParams(dimension_semantics=sem, vmem_limit_bytes=vmem_mib * MIB)


def _const_spec(shape):
    nd = len(shape)
    return pl.BlockSpec(shape, lambda *_: (0,) * nd)


def _rms_mod(x, g, shift, scale):
    return x * lax.rsqrt(jnp.mean(x * x, axis=-1, keepdims=True) + RMS_EPS) * (g * (1.0 + scale)) + shift


def _bdot(a, b):
    return jnp.dot(a, b, preferred_element_type=F32)


def _ada_kernel(c_ref, w_ref, b_ref, o_ref):
    c = c_ref[...]
    ca = c * jax.nn.sigmoid(c)
    o_ref[...] = _bdot(ca.astype(BF16), w_ref[...].astype(BF16)) + b_ref[...]


def _ada(c, w, b):
    bsz, d = c.shape
    n = w.shape[1]
    tn = min(n, 1536)
    return pl.pallas_call(
        _ada_kernel,
        grid=(n // tn,),
        in_specs=[_const_spec((bsz, d)),
                  pl.BlockSpec((d, tn), lambda j: (0, j)),
                  pl.BlockSpec((1, tn), lambda j: (0, j))],
        out_specs=pl.BlockSpec((bsz, tn), lambda j: (0, j)),
        out_shape=jax.ShapeDtypeStruct((bsz, n), F32),
        compiler_params=_cparams(("arbitrary",), 40),
        name="ada",
    )(c, w, b.reshape(1, n))


def _proj_cf_kernel(x_ref, mod_ref, g_ref, w_ref, b_ref, v_ref, f_ref, *, cw):
    u = _rms_mod(x_ref[0], g_ref[...], mod_ref[0, 0:1, :], mod_ref[0, 1:2, :])
    p = _bdot(u.astype(BF16), w_ref[...]) + b_ref[...]
    a = p[:, :cw]
    g = p[:, cw:2 * cw]
    v_ref[0] = (a * jax.nn.sigmoid(g)).astype(BF16)
    f_ref[0] = p[:, 2 * cw:].astype(BF16)


def _proj_cf(x, mod, g1, w_cf, b_cf, cw, fw, ts):
    bsz, s, d = x.shape
    n = w_cf.shape[1]
    tile = lambda m: pl.BlockSpec((1, ts, m), lambda b, j: (b, j, 0))
    return pl.pallas_call(
        functools.partial(_proj_cf_kernel, cw=cw),
        grid=(bsz, s // ts),
        in_specs=[tile(d),
                  pl.BlockSpec((1, N_MOD, d), lambda b, j: (b, 0, 0)),
                  _const_spec((1, d)),
                  _const_spec((d, n)),
                  _const_spec((1, n))],
        out_specs=[tile(cw), tile(fw)],
        out_shape=[jax.ShapeDtypeStruct((bsz, s, cw), BF16),
                   jax.ShapeDtypeStruct((bsz, s, fw), BF16)],
        compiler_params=_cparams(("arbitrary", "arbitrary"), 40),
        name="proj_cf",
    )(x, mod, g1, w_cf, b_cf)


def _group_block(fw):
    gd = fw // FOURIER_GROUPS
    return gd * max(1, min(FOURIER_GROUPS, LANES // gd))


def _dft_constants(s, fw):
    gd = fw // FOURIER_GROUPS
    q = s // DFT_RADIX
    j = np.arange(gd)
    ang = 2.0 * np.pi * np.outer(j, j) / gd
    scale = 1.0 / np.sqrt(float(s) * gd)
    wb = _group_block(fw)
    wr = np.zeros((wb, wb), np.float64)
    wi = np.zeros((wb, wb), np.float64)
    for g in range(wb // gd):
        sl = slice(g * gd, (g + 1) * gd)
        wr[sl, sl] = np.cos(ang) * scale
        wi[sl, sl] = -np.sin(ang) * scale
    wg = np.concatenate([wr, wi], axis=1).astype(np.float32)
    b = np.arange(q, dtype=np.int64)
    k = (DFT_RADIX * np.arange(q, dtype=np.int64)[None, :, None]
         + np.arange(DFT_RADIX, dtype=np.int64)[:, None, None])
    m = (k * b[None, None, :]) % s
    ang2 = 2.0 * np.pi * m.astype(np.float64) / s
    tab = np.concatenate([np.cos(ang2), np.sin(ang2)], axis=2).astype(np.float32)
    return wg, tab


def _fourier_kernel(f_ref, wg_ref, tab_ref, o_ref, a_ref, y_ref, *, q, fw, rc, pk):
    wg = wg_ref[...]
    wb = wg.shape[0]
    for c in range(q // rc):
        zr, zi = [], []
        for a in range(DFT_RADIX):
            fa = f_ref[0, a * q + c * rc:a * q + (c + 1) * rc, :]
            z = [_bdot(fa[:, g * wb:(g + 1) * wb], wg) for g in range(fw // wb)]
            zr.append(jnp.concatenate([t[:, :wb] for t in z], axis=1))
            zi.append(jnp.concatenate([t[:, wb:] for t in z], axis=1))
        er, ei = zr[0] + zr[2], zi[0] + zi[2]
        orr, oi = zr[1] + zr[3], zi[1] + zi[3]
        dr, di = zr[0] - zr[2], zi[0] - zi[2]
        pr, pi_ = zr[1] - zr[3], zi[1] - zi[3]
        ar = [er + orr, dr + pi_, er - orr, dr - pi_]
        ai = [ei + oi, di - pr, ei - oi, di + pr]
        for k1 in range(DFT_RADIX):
            a_ref[k1, c * rc:(c + 1) * rc, :] = ar[k1].astype(BF16)
            a_ref[k1, q + c * rc:q + (c + 1) * rc, :] = ai[k1].astype(BF16)
    for k1 in range(DFT_RADIX):
        y_ref[k1] = _bdot(tab_ref[k1], a_ref[k1]).astype(BF16)
    pt = DFT_RADIX * pk
    r_i = lax.broadcasted_iota(I32, (pt, pt), 0)
    c_i = lax.broadcasted_iota(I32, (pt, pt), 1)
    src = DFT_RADIX * jnp.bitwise_and(c_i, pk - 1) + jnp.right_shift(c_i, pk.bit_length() - 1)
    perm = jnp.where(r_i == src, 1.0, 0.0).astype(BF16)
    for t in range(q // pk):
        yc = jnp.concatenate([y_ref[k1, t * pk:(t + 1) * pk, :] for k1 in range(DFT_RADIX)], axis=0)
        o_ref[0, t * pt:(t + 1) * pt, :] = _bdot(perm, yc).astype(BF16)


def _fourier(f, wg, tab):
    bsz, s, fw = f.shape
    q = s // DFT_RADIX
    rc = min(q, 256)
    pk = min(q, LANES)
    assert pk & (pk - 1) == 0 and q % pk == 0
    return pl.pallas_call(
        functools.partial(_fourier_kernel, q=q, fw=fw, rc=rc, pk=pk),
        grid=(bsz,),
        in_specs=[pl.BlockSpec((1, s, fw), lambda b: (b, 0, 0)),
                  pl.BlockSpec(wg.shape, lambda b: (0, 0), pipeline_mode=pl.Buffered(1)),
                  pl.BlockSpec((DFT_RADIX, q, 2 * q), lambda b: (0, 0, 0), pipeline_mode=pl.Buffered(1))],
        out_specs=pl.BlockSpec((1, s, fw), lambda b: (b, 0, 0)),
        out_shape=jax.ShapeDtypeStruct((bsz, s, fw), BF16),
        scratch_shapes=[pltpu.VMEM((DFT_RADIX, 2 * q, fw), BF16),
                        pltpu.VMEM((DFT_RADIX, q, fw), BF16)],
        compiler_params=_cparams(("arbitrary",), 56),
        name="fourier",
    )(f, wg, tab)


def _mixer_kernel(x_ref, mod_ref, g1_ref, wg_ref, bg_ref, v_ref, dww_ref, dwb_ref, lng_ref, lnb_ref,
                  wco_ref, bco_ref, fr_ref, wf_ref, bf_ref, wo_ref, bo_ref, g2_ref, rw_ref,
                  h_ref, u2_ref, lg_ref, vwin_ref, *, ts, s, d, kw, halo):
    j = pl.program_id(1)
    nj = pl.num_programs(1)
    s0 = pl.multiple_of(j * ts, ts)
    pad = kw // 2
    mod = mod_ref[0]

    vwin_ref[halo:halo + ts, :] = v_ref[0, pl.ds(s0, ts), :].astype(F32)
    lo = pl.multiple_of(jnp.maximum(s0 - halo, 0), halo)
    vwin_ref[0:halo, :] = jnp.where(j > 0, v_ref[0, pl.ds(lo, halo), :].astype(F32), 0.0)
    hi = pl.multiple_of(jnp.minimum(s0 + ts, s - halo), halo)
    vwin_ref[halo + ts:halo + ts + halo, :] = jnp.where(
        j < nj - 1, v_ref[0, pl.ds(hi, halo), :].astype(F32), 0.0)

    first = halo - pad
    x = x_ref[0]
    u = _rms_mod(x, g1_ref[...], mod[0:1], mod[1:2])
    gates = jax.nn.sigmoid(_bdot(u.astype(BF16), wg_ref[...]) + bg_ref[...])
    acc = None
    for r in range(SUBLANES):
        part = None
        for qq in range((first + kw + SUBLANES - 1) // SUBLANES):
            o = SUBLANES * qq + r
            if first <= o < first + kw:
                win = vwin_ref[SUBLANES * qq:SUBLANES * qq + ts + SUBLANES, :]
                term = win * dww_ref[o - first:o - first + 1, :]
                part = term if part is None else part + term
        if part is None:
            continue
        shifted = part[:ts] if r == 0 else pltpu.roll(part, ts + SUBLANES - r, axis=0)[:ts]
        acc = shifted + dwb_ref[...] if acc is None else acc + shifted
    mu = jnp.mean(acc, axis=-1, keepdims=True)
    xc = acc - mu
    var = jnp.mean(xc * xc, axis=-1, keepdims=True)
    cv = xc * lax.rsqrt(var + LN_EPS) * lng_ref[...] + lnb_ref[...]
    cv = cv * jax.nn.sigmoid(cv)
    y_conv = _bdot(cv.astype(BF16), wco_ref[...]) + bco_ref[...]
    y_four = _bdot(fr_ref[0], wf_ref[...]) + bf_ref[...]
    merged = gates[:, :d] * y_conv + gates[:, d:] * y_four
    o_proj = _bdot(merged.astype(BF16), wo_ref[...]) + bo_ref[...]
    h = x + mod[2:3] * o_proj
    h_ref[0] = h
    u2b = _rms_mod(h, g2_ref[...], mod[3:4], mod[4:5]).astype(BF16)
    u2_ref[0] = u2b
    lg_ref[0] = _bdot(u2b, rw_ref[...])


def _mixer(x, mod, g1, w_g, b_g, v, dww, dwb, lng, lnb, wco, bco, fr, wf, bf, wo, bo, g2, rw, ts):
    bsz, s, d = x.shape
    cw = v.shape[2]
    fw = fr.shape[2]
    kw = dww.shape[0]
    halo = BF16_ROWS
    assert kw // 2 <= halo and s % ts == 0 and ts % halo == 0
    tile = lambda n: pl.BlockSpec((1, ts, n), lambda b, j: (b, j, 0))
    return pl.pallas_call(
        functools.partial(_mixer_kernel, ts=ts, s=s, d=d, kw=kw, halo=halo),
        grid=(bsz, s // ts),
        in_specs=[tile(d),
                  pl.BlockSpec((1, N_MOD, d), lambda b, j: (b, 0, 0)),
                  _const_spec((1, d)),
                  _const_spec((d, 2 * d)), _const_spec((1, 2 * d)),
                  pl.BlockSpec((1, s, cw), lambda b, j: (b, 0, 0)),
                  _const_spec((kw, cw)), _const_spec((1, cw)), _const_spec((1, cw)), _const_spec((1, cw)),
                  _const_spec((cw, d)), _const_spec((1, d)),
                  tile(fw),
                  _const_spec((fw, d)), _const_spec((1, d)),
                  _const_spec((d, d)), _const_spec((1, d)),
                  _const_spec((1, d)),
                  _const_spec((d, LANES))],
        out_specs=[tile(d), tile(d), tile(LANES)],
        out_shape=[jax.ShapeDtypeStruct((bsz, s, d), F32),
                   jax.ShapeDtypeStruct((bsz, s, d), BF16),
                   jax.ShapeDtypeStruct((bsz, s, LANES), F32)],
        scratch_shapes=[pltpu.VMEM((ts + 2 * halo, cw), F32)],
        compiler_params=_cparams(("arbitrary", "arbitrary"), 56),
        name="mixer",
    )(x, mod, g1, w_g, b_g, v, dww, dwb, lng, lnb, wco, bco, fr, wf, bf, wo, bo, g2, rw)


def _route_kernel(lg_ref, grank_ref, aff_ref, offs_ref, *, ne, cap, s, tt):
    lt = lg_ref[0].T[:ne, :]
    m = jnp.max(lt, axis=0, keepdims=True)
    e = jnp.exp(lt - m)
    aff = e / jnp.sum(e, axis=0, keepdims=True)
    aff_ref[0] = aff

    def enough(cand_bits):
        cnt = jnp.sum(jnp.where(aff >= pltpu.bitcast(cand_bits, F32), 1.0, 0.0), axis=1, keepdims=True)
        return cnt >= cap

    def search(i, thr_bits):
        hi = jnp.left_shift(jnp.int32(1), 29 - 2 * i)
        lo = jnp.left_shift(jnp.int32(1), 28 - 2 * i)
        c_hi, c_lo, c_both = thr_bits | hi, thr_bits | lo, thr_bits | hi | lo
        return jnp.where(enough(c_both), c_both,
                         jnp.where(enough(c_hi), c_hi, jnp.where(enough(c_lo), c_lo, thr_bits)))

    thr = pltpu.bitcast(lax.fori_loop(0, 15, search, jnp.zeros((ne, 1), I32)), F32)
    gt = aff > thr
    tie = aff == thr
    need = cap - jnp.sum(jnp.where(gt, 1.0, 0.0), axis=1, keepdims=True)

    r_i = lax.broadcasted_iota(I32, (tt, tt), 0)
    c_i = lax.broadcasted_iota(I32, (tt, tt), 1)
    upper = jnp.where(r_i < c_i, 1.0, 0.0).astype(BF16)
    offs_ref[0] = jnp.zeros(offs_ref.shape[1:], I32)
    carry_tie = jnp.zeros((ne, 1), F32)
    carry_sel = jnp.zeros((ne, 1), F32)
    for j in range(s // tt):
        sl = slice(j * tt, (j + 1) * tt)
        tie_f = jnp.where(tie[:, sl], 1.0, 0.0)
        tie_rank = _bdot(tie_f.astype(BF16), upper) + carry_tie
        carry_tie = carry_tie + jnp.sum(tie_f, axis=1, keepdims=True)
        sel_f = jnp.where(gt[:, sl], 1.0, jnp.where(tie_rank < need, tie_f, 0.0))
        rank = _bdot(sel_f.astype(BF16), upper) + carry_sel
        offs_ref[0, :, j:j + 1] = carry_sel.astype(I32)
        carry_sel = carry_sel + jnp.sum(sel_f, axis=1, keepdims=True)
        grank_ref[0, :, sl] = jnp.where(sel_f > 0.5, rank.astype(I32), -1)
    offs_ref[0, :, s // tt:s // tt + 1] = carry_sel.astype(I32)


def _route(lg, ne, cap, tt):
    bsz, s, _ = lg.shape
    return pl.pallas_call(
        functools.partial(_route_kernel, ne=ne, cap=cap, s=s, tt=tt),
        grid=(bsz,),
        in_specs=[pl.BlockSpec((1, s, LANES), lambda b: (b, 0, 0))],
        out_specs=[pl.BlockSpec((1, ne, s), lambda b: (b, 0, 0)),
                   pl.BlockSpec((1, ne, s), lambda b: (b, 0, 0)),
                   pl.BlockSpec((1, ne, LANES), lambda b: (b, 0, 0))],
        out_shape=[jax.ShapeDtypeStruct((bsz, ne, s), I32),
                   jax.ShapeDtypeStruct((bsz, ne, s), F32),
                   jax.ShapeDtypeStruct((bsz, ne, LANES), I32)],
        compiler_params=_cparams(("arbitrary",), 40),
        name="route",
    )(lg)


def _round_plan(offs_ref, b, j, ne, nt, cap):
    base = [(b * ne + e) * (nt + 1) + j for e in range(ne)]
    off = [offs_ref[base[e]] for e in range(ne)]
    end = [offs_ref[base[e] + 1] for e in range(ne)]
    w0 = [(off[e] // BF16_ROWS) * BF16_ROWS for e in range(ne)]
    nr = jnp.int32(0)
    for e in range(ne):
        nr_e = jnp.where(end[e] > off[e], (end[e] - w0[e] + SLOT_WIN - 1) // SLOT_WIN, 0)
        nr = jnp.maximum(nr, nr_e)
    return w0, nr


def _one_hot_windows(gr, w0, r, ne, tt, cap):
    srow = lax.broadcasted_iota(I32, (SLOT_WIN, tt), 0)
    starts, masks = [], []
    for e in range(ne):
        lo = w0[e] + SLOT_WIN * r
        start = pl.multiple_of(jnp.minimum(lo, cap - SLOT_WIN), BF16_ROWS)
        g = gr[e:e + 1, :]
        g = jnp.where(g >= lo, g, -1)
        masks.append((g - start) == srow)
        starts.append(start)
    return starts, masks


def _gather_kernel(offs_ref, u2_ref, grank_ref, aff_ref, xg_ref, wrep_ref, *, ne, nt, tt, cap, nsub):
    b = pl.program_id(0)
    j = pl.program_id(1)

    @pl.when(j == 0)
    def _():
        xg_ref[...] = jnp.zeros(xg_ref.shape, BF16)
        wrep_ref[...] = jnp.zeros(wrep_ref.shape, F32)

    plans = [_round_plan(offs_ref, b, j * nsub + sub, ne, nt, cap) for sub in range(nsub)]

    def do_round(sub, r):
        cols = slice(sub * tt, (sub + 1) * tt)
        starts, masks = _one_hot_windows(grank_ref[0, :, cols], plans[sub][0], r, ne, tt, cap)
        pcat = jnp.concatenate([jnp.where(mk, 1.0, 0.0).astype(BF16) for mk in masks], axis=0)
        xw = _bdot(pcat, u2_ref[0, cols, :]).astype(BF16)
        af = aff_ref[0, :, cols]
        for e in range(ne):
            rows = pl.ds(starts[e], SLOT_WIN)
            xg_ref[0, e, rows, :] = xg_ref[0, e, rows, :] + xw[e * SLOT_WIN:(e + 1) * SLOT_WIN]
            ws = jnp.sum(jnp.where(masks[e], af[e:e + 1, :], 0.0), axis=1, keepdims=True)
            wrep_ref[0, e, rows, :] = wrep_ref[0, e, rows, :] + jnp.broadcast_to(ws, (SLOT_WIN, LANES))

    for sub in range(nsub):
        do_round(sub, 0)
    for sub in range(nsub):
        def extra(r, carry, sub=sub):
            do_round(sub, r)
            return carry
        lax.fori_loop(1, plans[sub][1], extra, 0)


def _gather(offs, u2, grank, aff, ne, cap, tt, nsub):
    bsz, s, d = u2.shape
    nt = s // tt
    tg = nsub * tt
    grid_spec = pltpu.PrefetchScalarGridSpec(
        num_scalar_prefetch=1,
        grid=(bsz, nt // nsub),
        in_specs=[pl.BlockSpec((1, tg, d), lambda b, j, o: (b, j, 0)),
                  pl.BlockSpec((1, ne, tg), lambda b, j, o: (b, 0, j)),
                  pl.BlockSpec((1, ne, tg), lambda b, j, o: (b, 0, j))],
        out_specs=[pl.BlockSpec((1, ne, cap, d), lambda b, j, o: (b, 0, 0, 0)),
                   pl.BlockSpec((1, ne, cap, LANES), lambda b, j, o: (b, 0, 0, 0))],
    )
    return pl.pallas_call(
        functools.partial(_gather_kernel, ne=ne, nt=nt, tt=tt, cap=cap, nsub=nsub),
        grid_spec=grid_spec,
        out_shape=[jax.ShapeDtypeStruct((bsz, ne, cap, d), BF16),
                   jax.ShapeDtypeStruct((bsz, ne, cap, LANES), F32)],
        compiler_params=_cparams(("arbitrary", "arbitrary"), 56),
        name="gather",
    )(offs, u2, grank, aff)


def _ffn_kernel(x_ref, wg_ref, wu_ref, wd_ref, wrep_ref, o_ref, acc_ref, *, nb, cap, d):
    f = pl.program_id(2)
    last = pl.num_programs(2) - 1
    wg = wg_ref[0].astype(BF16)
    wu = wu_ref[0].astype(BF16)
    wd = wd_ref[0].astype(BF16)

    def partial_out(i):
        x = x_ref[i, 0]
        hg = _bdot(x, wg)
        hu = _bdot(x, wu)
        act = (hg * jax.nn.sigmoid(hg) * hu).astype(BF16)
        return _bdot(act, wd)

    @pl.when(f == 0)
    def _():
        for i in range(nb):
            acc_ref[i] = partial_out(i)

    @pl.when(jnp.logical_and(f > 0, f < last))
    def _():
        for i in range(nb):
            acc_ref[i] += partial_out(i)

    @pl.when(f == last)
    def _():
        for i in range(nb):
            tot = acc_ref[i] + partial_out(i)
            w = wrep_ref[i, 0]
            for c in range(d // LANES):
                cols = slice(c * LANES, (c + 1) * LANES)
                o_ref[i, 0, :, cols] = (tot[:, cols] * w).astype(BF16)


def _ffn(xg, wrep, w_gate, w_up, w_down, nb, tf):
    bsz, ne, cap, d = xg.shape
    ff = w_gate.shape[2]
    assert ff % tf == 0 and ff // tf >= 2
    return pl.pallas_call(
        functools.partial(_ffn_kernel, nb=nb, cap=cap, d=d),
        grid=(ne, bsz // nb, ff // tf),
        in_specs=[pl.BlockSpec((nb, 1, cap, d), lambda e, m, f: (m, e, 0, 0)),
                  pl.BlockSpec((1, d, tf), lambda e, m, f: (e, 0, f)),
                  pl.BlockSpec((1, d, tf), lambda e, m, f: (e, 0, f)),
                  pl.BlockSpec((1, tf, d), lambda e, m, f: (e, f, 0)),
                  pl.BlockSpec((nb, 1, cap, LANES), lambda e, m, f: (m, e, 0, 0))],
        out_specs=pl.BlockSpec((nb, 1, cap, d), lambda e, m, f: (m, e, 0, 0)),
        out_shape=jax.ShapeDtypeStruct((bsz, ne, cap, d), BF16),
        scratch_shapes=[pltpu.VMEM((nb, cap, d), F32)],
        compiler_params=_cparams(("arbitrary", "arbitrary", "arbitrary"), 56),
        name="ffn",
    )(xg, w_gate, w_up, w_down, wrep)


def _combine_kernel(offs_ref, grank_ref, eo_ref, h_ref, mod_ref, gf_ref, o_ref, y_ref, *,
                    ne, nt, tt, cap, nsub, final):
    b = pl.program_id(0)
    j = pl.program_id(1)
    plans = [_round_plan(offs_ref, b, j * nsub + sub, ne, nt, cap) for sub in range(nsub)]

    def round_sum(sub, r):
        cols = slice(sub * tt, (sub + 1) * tt)
        starts, masks = _one_hot_windows(grank_ref[0, :, cols], plans[sub][0], r, ne, tt, cap)
        pcat = jnp.concatenate([jnp.where(mk, 1.0, 0.0).astype(BF16) for mk in masks], axis=0)
        ocat = jnp.concatenate([eo_ref[0, e, pl.ds(starts[e], SLOT_WIN), :] for e in range(ne)], axis=0)
        return lax.dot_general(pcat, ocat, (((0,), (0,)), ((), ())), preferred_element_type=F32)

    for sub in range(nsub):
        y_ref[sub * tt:(sub + 1) * tt, :] = round_sum(sub, 0)
    for sub in range(nsub):
        def extra(r, carry, sub=sub):
            y_ref[sub * tt:(sub + 1) * tt, :] += round_sum(sub, r)
            return carry
        lax.fori_loop(1, plans[sub][1], extra, 0)
    hout = h_ref[0] + mod_ref[0, 5:6, :] * y_ref[...]
    if final:
        hout = hout * lax.rsqrt(jnp.mean(hout * hout, axis=-1, keepdims=True) + RMS_EPS) * gf_ref[...]
    o_ref[0] = hout


def _combine(offs, grank, eo, h, mod, gf, tt, nsub, final):
    bsz, s, d = h.shape
    ne, cap = eo.shape[1], eo.shape[2]
    nt = s // tt
    tg = nsub * tt
    grid_spec = pltpu.PrefetchScalarGridSpec(
        num_scalar_prefetch=1,
        grid=(bsz, nt // nsub),
        in_specs=[pl.BlockSpec((1, ne, tg), lambda b, j, o: (b, 0, j)),
                  pl.BlockSpec((1, ne, cap, d), lambda b, j, o: (b, 0, 0, 0)),
                  pl.BlockSpec((1, tg, d), lambda b, j, o: (b, j, 0)),
                  pl.BlockSpec((1, N_MOD, d), lambda b, j, o: (b, 0, 0)),
                  pl.BlockSpec((1, d), lambda b, j, o: (0, 0))],
        out_specs=pl.BlockSpec((1, tg, d), lambda b, j, o: (b, j, 0)),
        scratch_shapes=[pltpu.VMEM((tg, d), F32)],
    )
    return pl.pallas_call(
        functools.partial(_combine_kernel, ne=ne, nt=nt, tt=tt, cap=cap, nsub=nsub, final=final),
        grid_spec=grid_spec,
        out_shape=jax.ShapeDtypeStruct((bsz, s, d), F32),
        compiler_params=_cparams(("arbitrary", "arbitrary"), 56),
        name="combine",
    )(offs, grank, eo, h, mod, gf)


def kernel(x, c, ada_w, ada_b, norm1_g, w_in, b_in, conv_dw_w, conv_dw_b, conv_ln_g, conv_ln_b,
           conv_w_out, conv_b_out, fourier_w, fourier_b, w_out, b_out, norm2_g, router_w,
           expert_w_gate, expert_w_up, expert_w_down, final_norm_g):
    bsz, s, d = x.shape
    depth = ada_w.shape[0]
    cw = conv_dw_w.shape[2]
    fw = fourier_w.shape[1]
    ne = router_w.shape[2]
    cap = CAPACITY_FACTOR * s // ne
    tt = min(ROUTE_TILE, s)
    ts = min(512, s)
    assert s % DFT_RADIX == 0 and s % tt == 0 and cap % BF16_ROWS == 0 and cap >= SLOT_WIN and ne <= LANES
    wg_np, tab_np = _dft_constants(s, fw)
    dft_wg = jnp.asarray(wg_np).astype(BF16)
    dft_tab = jnp.asarray(tab_np).astype(BF16)
    row = lambda a: a.reshape(1, -1)

    h = x
    for l in range(depth):
        mod = _ada(c, ada_w[l], ada_b[l]).reshape(bsz, N_MOD, d)
        c1 = 2 * cw + fw
        v, f = _proj_cf(h, mod, row(norm1_g[l]), w_in[l][:, :c1].astype(BF16), row(b_in[l][:c1]), cw, fw,
                        min(2 * ts, s))
        fr = _fourier(f, dft_wg, dft_tab)
        rw = jnp.zeros((d, LANES), BF16).at[:, :ne].set(router_w[l].astype(BF16))
        h, u2, lg = _mixer(h, mod, row(norm1_g[l]), w_in[l][:, c1:].astype(BF16), row(b_in[l][c1:]), v,
                           conv_dw_w[l], row(conv_dw_b[l]), row(conv_ln_g[l]), row(conv_ln_b[l]),
                           conv_w_out[l].astype(BF16), row(conv_b_out[l]), fr,
                           fourier_w[l].astype(BF16), row(fourier_b[l]),
                           w_out[l].astype(BF16), row(b_out[l]), row(norm2_g[l]), rw, ts)
        grank, aff, offs = _route(lg, ne, cap, tt)
        offs_flat = offs[:, :, :s // tt + 1].reshape(-1)
        nsub = 2 if (s // tt) % 2 == 0 else 1
        xg, wrep = _gather(offs_flat, u2, grank, aff, ne, cap, tt, nsub)
        nb = 4 if bsz % 4 == 0 else 1
        eo = _ffn(xg, wrep, expert_w_gate[l], expert_w_up[l], expert_w_down[l], nb, min(512, expert_w_gate.shape[3] // 2))
        h = _combine(offs_flat, grank, eo, h, mod, row(final_norm_g), tt, nsub, final=(l == depth - 1))
    return h
```

```python
import functools

import numpy as np
import jax
import jax.numpy as jnp
from jax import lax
from jax.experimental import pallas as pl
from jax.experimental.pallas import tpu as pltpu

F32 = jnp.float32
BF16 = jnp.bfloat16
I32 = jnp.int32

RMS_EPS = 1e-6
LN_EPS = 1e-5
FOURIER_GROUPS = 4
CAPACITY_FACTOR = 2
N_MOD = 6

LANES = 128
SUBLANES = 8
BF16_ROWS = 16
DFT_RADIX = 4
ROUTE_TILE = 256
SLOT_WIN = 64
MIB = 1024 * 1024


def _cparams(sem, vmem_mib, **kw):
    return pltpu.CompilerParams(dimension_semantics=sem, vmem_limit_bytes=vmem_mib * MIB, **kw)


def _const_spec(shape):
    nd = len(shape)
    return pl.BlockSpec(shape, lambda *_: (0,) * nd)


def _rms_mod(x, g, shift, scale):
    return x * lax.rsqrt(jnp.mean(x * x, axis=-1, keepdims=True) + RMS_EPS) * (g * (1.0 + scale)) + shift


def _bdot(a, b):
    return jnp.dot(a, b, preferred_element_type=F32)


def _ada_kernel(c_ref, w_ref, b_ref, o_ref):
    c = c_ref[...]
    ca = c * jax.nn.sigmoid(c)
    o_ref[...] = _bdot(ca.astype(BF16), w_ref[...].astype(BF16)) + b_ref[...]


def _ada(c, w, b):
    bsz, d = c.shape
    n = w.shape[1]
    tn = min(n, 1536)
    return pl.pallas_call(
        _ada_kernel,
        grid=(n // tn,),
        in_specs=[_const_spec((bsz, d)),
                  pl.BlockSpec((d, tn), lambda j: (0, j)),
                  pl.BlockSpec((1, tn), lambda j: (0, j))],
        out_specs=pl.BlockSpec((bsz, tn), lambda j: (0, j)),
        out_shape=jax.ShapeDtypeStruct((bsz, n), F32),
        compiler_params=_cparams(("arbitrary",), 40),
        name="ada",
    )(c, w, b.reshape(1, n))


def _proj_cf_kernel(x_ref, mod_ref, g_ref, w_ref, b_ref, v_ref, f_ref, *, cw):
    u = _rms_mod(x_ref[0], g_ref[...], mod_ref[0, 0:1, :], mod_ref[0, 1:2, :])
    p = _bdot(u.astype(BF16), w_ref[...]) + b_ref[...]
    a = p[:, :cw]
    g = p[:, cw:2 * cw]
    v_ref[0] = (a * jax.nn.sigmoid(g)).astype(BF16)
    f_ref[0] = p[:, 2 * cw:].astype(BF16)


def _proj_cf(x, mod, g1, w_cf, b_cf, cw, fw, ts):
    bsz, s, d = x.shape
    n = w_cf.shape[1]
    tile = lambda m: pl.BlockSpec((1, ts, m), lambda b, j: (b, j, 0))
    return pl.pallas_call(
        functools.partial(_proj_cf_kernel, cw=cw),
        grid=(bsz, s // ts),
        in_specs=[tile(d),
                  pl.BlockSpec((1, N_MOD, d), lambda b, j: (b, 0, 0)),
                  _const_spec((1, d)),
                  _const_spec((d, n)),
                  _const_spec((1, n))],
        out_specs=[tile(cw), tile(fw)],
        out_shape=[jax.ShapeDtypeStruct((bsz, s, cw), BF16),
                   jax.ShapeDtypeStruct((bsz, s, fw), BF16)],
        compiler_params=_cparams(("arbitrary", "arbitrary"), 40),
        name="proj_cf",
    )(x, mod, g1, w_cf, b_cf)


def _group_block(fw):
    gd = fw // FOURIER_GROUPS
    return gd * max(1, min(FOURIER_GROUPS, LANES // gd))


def _dft_constants(s, fw):
    gd = fw // FOURIER_GROUPS
    q = s // DFT_RADIX
    j = np.arange(gd)
    ang = 2.0 * np.pi * np.outer(j, j) / gd
    scale = 1.0 / np.sqrt(float(s) * gd)
    wb = _group_block(fw)
    wr = np.zeros((wb, wb), np.float64)
    wi = np.zeros((wb, wb), np.float64)
    for g in range(wb // gd):
        sl = slice(g * gd, (g + 1) * gd)
        wr[sl, sl] = np.cos(ang) * scale
        wi[sl, sl] = -np.sin(ang) * scale
    wg = np.concatenate([wr, wi], axis=1).astype(np.float32)
    b = np.arange(q, dtype=np.int64)
    k = (DFT_RADIX * np.arange(q, dtype=np.int64)[None, :, None]
         + np.arange(DFT_RADIX, dtype=np.int64)[:, None, None])
    m = (k * b[None, None, :]) % s
    ang2 = 2.0 * np.pi * m.astype(np.float64) / s
    tab = np.concatenate([np.cos(ang2), np.sin(ang2)], axis=2).astype(np.float32)
    return wg, tab


def _fourier_kernel(f_ref, wg_ref, tab_ref, o_ref, a_ref, y_ref, *, q, fw, rc, pk):
    wg = wg_ref[...]
    wb = wg.shape[0]
    for c in range(q // rc):
        zr, zi = [], []
        for a in range(DFT_RADIX):
            fa = f_ref[0, a * q + c * rc:a * q + (c + 1) * rc, :]
            z = [_bdot(fa[:, g * wb:(g + 1) * wb], wg) for g in range(fw // wb)]
            zr.append(jnp.concatenate([t[:, :wb] for t in z], axis=1))
            zi.append(jnp.concatenate([t[:, wb:] for t in z], axis=1))
        er, ei = zr[0] + zr[2], zi[0] + zi[2]
        orr, oi = zr[1] + zr[3], zi[1] + zi[3]
        dr, di = zr[0] - zr[2], zi[0] - zi[2]
        pr, pi_ = zr[1] - zr[3], zi[1] - zi[3]
        ar = [er + orr, dr + pi_, er - orr, dr - pi_]
        ai = [ei + oi, di - pr, ei - oi, di + pr]
        for k1 in range(DFT_RADIX):
            a_ref[k1, c * rc:(c + 1) * rc, :] = ar[k1].astype(BF16)
            a_ref[k1, q + c * rc:q + (c + 1) * rc, :] = ai[k1].astype(BF16)
    for k1 in range(DFT_RADIX):
        y_ref[k1] = _bdot(tab_ref[k1], a_ref[k1]).astype(BF16)
    pt = DFT_RADIX * pk
    r_i = lax.broadcasted_iota(I32, (pt, pt), 0)
    c_i = lax.broadcasted_iota(I32, (pt, pt), 1)
    src = DFT_RADIX * jnp.bitwise_and(c_i, pk - 1) + jnp.right_shift(c_i, pk.bit_length() - 1)
    perm = jnp.where(r_i == src, 1.0, 0.0).astype(BF16)
    for t in range(q // pk):
        yc = jnp.concatenate([y_ref[k1, t * pk:(t + 1) * pk, :] for k1 in range(DFT_RADIX)], axis=0)
        o_ref[0, t * pt:(t + 1) * pt, :] = _bdot(perm, yc).astype(BF16)


def _fourier(f, wg, tab):
    bsz, s, fw = f.shape
    q = s // DFT_RADIX
    rc = min(q, 256)
    pk = min(q, LANES)
    assert pk & (pk - 1) == 0 and q % pk == 0
    return pl.pallas_call(
        functools.partial(_fourier_kernel, q=q, fw=fw, rc=rc, pk=pk),
        grid=(bsz,),
        in_specs=[pl.BlockSpec((1, s, fw), lambda b: (b, 0, 0)),
                  pl.BlockSpec(wg.shape, lambda b: (0, 0), pipeline_mode=pl.Buffered(1)),
                  pl.BlockSpec((DFT_RADIX, q, 2 * q), lambda b: (0, 0, 0), pipeline_mode=pl.Buffered(1))],
        out_specs=pl.BlockSpec((1, s, fw), lambda b: (b, 0, 0)),
        out_shape=jax.ShapeDtypeStruct((bsz, s, fw), BF16),
        scratch_shapes=[pltpu.VMEM((DFT_RADIX, 2 * q, fw), BF16),
                        pltpu.VMEM((DFT_RADIX, q, fw), BF16)],
        compiler_params=_cparams(("arbitrary",), 56),
        name="fourier",
    )(f, wg, tab)


def _mixer_kernel(x_ref, mod_ref, g1_ref, wg_ref, bg_ref, v_ref, dww_ref, dwb_ref, lng_ref, lnb_ref,
                  wco_ref, bco_ref, fr_ref, wf_ref, bf_ref, wo_ref, bo_ref, g2_ref, rw_ref,
                  h_ref, u2_ref, lg_ref, vwin_ref, *, ts, s, d, kw, halo):
    j = pl.program_id(1)
    nj = pl.num_programs(1)
    s0 = pl.multiple_of(j * ts, ts)
    pad = kw // 2
    mod = mod_ref[0]

    vwin_ref[halo:halo + ts, :] = v_ref[0, pl.ds(s0, ts), :].astype(F32)
    lo = pl.multiple_of(jnp.maximum(s0 - halo, 0), halo)
    vwin_ref[0:halo, :] = jnp.where(j > 0, v_ref[0, pl.ds(lo, halo), :].astype(F32), 0.0)
    hi = pl.multiple_of(jnp.minimum(s0 + ts, s - halo), halo)
    vwin_ref[halo + ts:halo + ts + halo, :] = jnp.where(
        j < nj - 1, v_ref[0, pl.ds(hi, halo), :].astype(F32), 0.0)

    first = halo - pad
    x = x_ref[0]
    ub = _rms_mod(x, g1_ref[...], mod[0:1], mod[1:2]).astype(BF16)
    acc = None
    for r in range(SUBLANES):
        part = None
        for qq in range((first + kw + SUBLANES - 1) // SUBLANES):
            o = SUBLANES * qq + r
            if first <= o < first + kw:
                win = vwin_ref[SUBLANES * qq:SUBLANES * qq + ts + SUBLANES, :]
                term = win * dww_ref[o - first:o - first + 1, :]
                part = term if part is None else part + term
        if part is None:
            continue
        shifted = part[:ts] if r == 0 else pltpu.roll(part, ts + SUBLANES - r, axis=0)[:ts]
        acc = shifted + dwb_ref[...] if acc is None else acc + shifted
    mu = jnp.mean(acc, axis=-1, keepdims=True)
    xc = acc - mu
    var = jnp.mean(xc * xc, axis=-1, keepdims=True)
    cv = xc * lax.rsqrt(var + LN_EPS) * lng_ref[...] + lnb_ref[...]
    cv = cv * jax.nn.sigmoid(cv)
    cvb = cv.astype(BF16)
    fr = fr_ref[0]
    o_proj = bo_ref[...]
    dh = d // 2
    for hh in range(2):
        c0, c1 = hh * dh, (hh + 1) * dh
        g_conv = jax.nn.sigmoid(_bdot(ub, wg_ref[:, c0:c1]) + bg_ref[:, c0:c1])
        g_four = jax.nn.sigmoid(_bdot(ub, wg_ref[:, d + c0:d + c1]) + bg_ref[:, d + c0:d + c1])
        y_conv = _bdot(cvb, wco_ref[:, c0:c1]) + bco_ref[:, c0:c1]
        y_four = _bdot(fr, wf_ref[:, c0:c1]) + bf_ref[:, c0:c1]
        merged = (g_conv * y_conv + g_four * y_four).astype(BF16)
        o_proj = o_proj + _bdot(merged, wo_ref[c0:c1, :])
    h = x + mod[2:3] * o_proj
    h_ref[0] = h
    u2b = _rms_mod(h, g2_ref[...], mod[3:4], mod[4:5]).astype(BF16)
    u2_ref[0] = u2b
    lg_ref[0] = _bdot(u2b, rw_ref[...])


def _mixer(x, mod, g1, w_g, b_g, v, dww, dwb, lng, lnb, wco, bco, fr, wf, bf, wo, bo, g2, rw, ts):
    bsz, s, d = x.shape
    cw = v.shape[2]
    fw = fr.shape[2]
    kw = dww.shape[0]
    halo = BF16_ROWS
    assert kw // 2 <= halo and s % ts == 0 and ts % halo == 0
    tile = lambda n: pl.BlockSpec((1, ts, n), lambda b, j: (b, j, 0))
    return pl.pallas_call(
        functools.partial(_mixer_kernel, ts=ts, s=s, d=d, kw=kw, halo=halo),
        grid=(bsz, s // ts),
        in_specs=[tile(d),
                  pl.BlockSpec((1, N_MOD, d), lambda b, j: (b, 0, 0)),
                  _const_spec((1, d)),
                  _const_spec((d, 2 * d)), _const_spec((1, 2 * d)),
                  pl.BlockSpec((1, s, cw), lambda b, j: (b, 0, 0)),
                  _const_spec((kw, cw)), _const_spec((1, cw)), _const_spec((1, cw)), _const_spec((1, cw)),
                  _const_spec((cw, d)), _const_spec((1, d)),
                  tile(fw),
                  _const_spec((fw, d)), _const_spec((1, d)),
                  _const_spec((d, d)), _const_spec((1, d)),
                  _const_spec((1, d)),
                  _const_spec((d, LANES))],
        out_specs=[tile(d), tile(d), tile(LANES)],
        out_shape=[jax.ShapeDtypeStruct((bsz, s, d), F32),
                   jax.ShapeDtypeStruct((bsz, s, d), BF16),
                   jax.ShapeDtypeStruct((bsz, s, LANES), F32)],
        scratch_shapes=[pltpu.VMEM((ts + 2 * halo, cw), F32)],
        compiler_params=_cparams(("arbitrary", "arbitrary"), 56),
        name="mixer",
    )(x, mod, g1, w_g, b_g, v, dww, dwb, lng, lnb, wco, bco, fr, wf, bf, wo, bo, g2, rw)


def _route_kernel(lg_ref, grank_ref, aff_ref, offs_ref, *, ne, cap, s, tt):
    lt = lg_ref[0].T[:ne, :]
    m = jnp.max(lt, axis=0, keepdims=True)
    e = jnp.exp(lt - m)
    aff = e / jnp.sum(e, axis=0, keepdims=True)
    aff_ref[0] = aff

    def enough(cand_bits):
        cnt = jnp.sum(jnp.where(aff >= pltpu.bitcast(cand_bits, F32), 1.0, 0.0), axis=1, keepdims=True)
        return cnt >= cap

    def search(i, thr_bits):
        hi = jnp.left_shift(jnp.int32(1), 29 - 2 * i)
        lo = jnp.left_shift(jnp.int32(1), 28 - 2 * i)
        c_hi, c_lo, c_both = thr_bits | hi, thr_bits | lo, thr_bits | hi | lo
        return jnp.where(enough(c_both), c_both,
                         jnp.where(enough(c_hi), c_hi, jnp.where(enough(c_lo), c_lo, thr_bits)))

    thr = pltpu.bitcast(lax.fori_loop(0, 15, search, jnp.zeros((ne, 1), I32)), F32)
    gt = aff > thr
    tie = aff == thr
    need = cap - jnp.sum(jnp.where(gt, 1.0, 0.0), axis=1, keepdims=True)

    r_i = lax.broadcasted_iota(I32, (tt, tt), 0)
    c_i = lax.broadcasted_iota(I32, (tt, tt), 1)
    upper = jnp.where(r_i < c_i, 1.0, 0.0).astype(BF16)
    offs_ref[0] = jnp.zeros(offs_ref.shape[1:], I32)
    carry_tie = jnp.zeros((ne, 1), F32)
    carry_sel = jnp.zeros((ne, 1), F32)
    for j in range(s // tt):
        sl = slice(j * tt, (j + 1) * tt)
        tie_f = jnp.where(tie[:, sl], 1.0, 0.0)
        tie_rank = _bdot(tie_f.astype(BF16), upper) + carry_tie
        carry_tie = carry_tie + jnp.sum(tie_f, axis=1, keepdims=True)
        sel_f = jnp.where(gt[:, sl], 1.0, jnp.where(tie_rank < need, tie_f, 0.0))
        rank = _bdot(sel_f.astype(BF16), upper) + carry_sel
        offs_ref[0, :, j:j + 1] = carry_sel.astype(I32)
        carry_sel = carry_sel + jnp.sum(sel_f, axis=1, keepdims=True)
        grank_ref[0, :, sl] = jnp.where(sel_f > 0.5, rank.astype(I32), -1)
    offs_ref[0, :, s // tt:s // tt + 1] = carry_sel.astype(I32)


def _route(lg, ne, cap, tt):
    bsz, s, _ = lg.shape
    return pl.pallas_call(
        functools.partial(_route_kernel, ne=ne, cap=cap, s=s, tt=tt),
        grid=(bsz,),
        in_specs=[pl.BlockSpec((1, s, LANES), lambda b: (b, 0, 0))],
        out_specs=[pl.BlockSpec((1, ne, s), lambda b: (b, 0, 0)),
                   pl.BlockSpec((1, ne, s), lambda b: (b, 0, 0)),
                   pl.BlockSpec((1, ne, LANES), lambda b: (b, 0, 0))],
        out_shape=[jax.ShapeDtypeStruct((bsz, ne, s), I32),
                   jax.ShapeDtypeStruct((bsz, ne, s), F32),
                   jax.ShapeDtypeStruct((bsz, ne, LANES), I32)],
        compiler_params=_cparams(("arbitrary",), 40),
        name="route",
    )(lg)


def _round_plan(offs_ref, b, j, ne, nt, cap):
    base = [(b * ne + e) * (nt + 1) + j for e in range(ne)]
    off = [offs_ref[base[e]] for e in range(ne)]
    end = [offs_ref[base[e] + 1] for e in range(ne)]
    w0 = [(off[e] // BF16_ROWS) * BF16_ROWS for e in range(ne)]
    nr = jnp.int32(0)
    for e in range(ne):
        nr_e = jnp.where(end[e] > off[e], (end[e] - w0[e] + SLOT_WIN - 1) // SLOT_WIN, 0)
        nr = jnp.maximum(nr, nr_e)
    return w0, nr


def _one_hot_windows(gr, w0, r, ne, tt, cap):
    srow = lax.broadcasted_iota(I32, (SLOT_WIN, tt), 0)
    starts, masks = [], []
    for e in range(ne):
        lo = w0[e] + SLOT_WIN * r
        start = pl.multiple_of(jnp.minimum(lo, cap - SLOT_WIN), BF16_ROWS)
        g = gr[e:e + 1, :]
        g = jnp.where(g >= lo, g, -1)
        masks.append((g - start) == srow)
        starts.append(start)
    return starts, masks


def _gather_kernel(offs_ref, u2_ref, grank_ref, aff_ref, xg_ref, wrep_ref, *, ne, nt, tt, cap, nsub):
    b = pl.program_id(0)
    j = pl.program_id(1)

    @pl.when(j == 0)
    def _():
        xg_ref[...] = jnp.zeros(xg_ref.shape, BF16)
        wrep_ref[...] = jnp.zeros(wrep_ref.shape, F32)

    plans = [_round_plan(offs_ref, b, j * nsub + sub, ne, nt, cap) for sub in range(nsub)]

    def do_round(sub, r):
        cols = slice(sub * tt, (sub + 1) * tt)
        starts, masks = _one_hot_windows(grank_ref[0, :, cols], plans[sub][0], r, ne, tt, cap)
        pcat = jnp.concatenate([jnp.where(mk, 1.0, 0.0).astype(BF16) for mk in masks], axis=0)
        xw = _bdot(pcat, u2_ref[0, cols, :]).astype(BF16)
        af = aff_ref[0, :, cols]
        for e in range(ne):
            rows = pl.ds(starts[e], SLOT_WIN)
            xg_ref[0, e, rows, :] = xg_ref[0, e, rows, :] + xw[e * SLOT_WIN:(e + 1) * SLOT_WIN]
            ws = jnp.sum(jnp.where(masks[e], af[e:e + 1, :], 0.0), axis=1, keepdims=True)
            wrep_ref[0, e, rows, :] = wrep_ref[0, e, rows, :] + jnp.broadcast_to(ws, (SLOT_WIN, LANES))

    for sub in range(nsub):
        do_round(sub, 0)
    for sub in range(nsub):
        def extra(r, carry, sub=sub):
            do_round(sub, r)
            return carry
        lax.fori_loop(1, plans[sub][1], extra, 0)


def _gather(offs, u2, grank, aff, ne, cap, tt, nsub):
    bsz, s, d = u2.shape
    nt = s // tt
    tg = nsub * tt
    grid_spec = pltpu.PrefetchScalarGridSpec(
        num_scalar_prefetch=1,
        grid=(bsz, nt // nsub),
        in_specs=[pl.BlockSpec((1, tg, d), lambda b, j, o: (b, j, 0)),
                  pl.BlockSpec((1, ne, tg), lambda b, j, o: (b, 0, j)),
                  pl.BlockSpec((1, ne, tg), lambda b, j, o: (b, 0, j))],
        out_specs=[pl.BlockSpec((1, ne, cap, d), lambda b, j, o: (b, 0, 0, 0)),
                   pl.BlockSpec((1, ne, cap, LANES), lambda b, j, o: (b, 0, 0, 0))],
    )
    return pl.pallas_call(
        functools.partial(_gather_kernel, ne=ne, nt=nt, tt=tt, cap=cap, nsub=nsub),
        grid_spec=grid_spec,
        out_shape=[jax.ShapeDtypeStruct((bsz, ne, cap, d), BF16),
                   jax.ShapeDtypeStruct((bsz, ne, cap, LANES), F32)],
        compiler_params=_cparams(("arbitrary", "arbitrary"), 56),
        name="gather",
    )(offs, u2, grank, aff)


def _ffn_kernel(x_ref, wg_ref, wu_ref, wd_ref, wrep_ref, o_ref, acc_ref, *, nb, cap, d):
    f = pl.program_id(2)
    last = pl.num_programs(2) - 1
    wg = wg_ref[0].astype(BF16)
    wu = wu_ref[0].astype(BF16)
    wd = wd_ref[0].astype(BF16)

    def partial_out(i):
        x = x_ref[i, 0]
        hg = _bdot(x, wg)
        hu = _bdot(x, wu)
        act = (hg * jax.nn.sigmoid(hg) * hu).astype(BF16)
        return _bdot(act, wd)

    @pl.when(f == 0)
    def _():
        for i in range(nb):
            acc_ref[i] = partial_out(i)

    @pl.when(jnp.logical_and(f > 0, f < last))
    def _():
        for i in range(nb):
            acc_ref[i] += partial_out(i)

    @pl.when(f == last)
    def _():
        for i in range(nb):
            tot = acc_ref[i] + partial_out(i)
            w = wrep_ref[i, 0]
            for c in range(d // LANES):
                cols = slice(c * LANES, (c + 1) * LANES)
                o_ref[i, 0, :, cols] = (tot[:, cols] * w).astype(BF16)


def _ffn(xg, wrep, w_gate, w_up, w_down, nb, tf):
    bsz, ne, cap, d = xg.shape
    ff = w_gate.shape[2]
    assert ff % tf == 0 and ff // tf >= 2
    return pl.pallas_call(
        functools.partial(_ffn_kernel, nb=nb, cap=cap, d=d),
        grid=(ne, bsz // nb, ff // tf),
        in_specs=[pl.BlockSpec((nb, 1, cap, d), lambda e, m, f: (m, e, 0, 0)),
                  pl.BlockSpec((1, d, tf), lambda e, m, f: (e, 0, f)),
                  pl.BlockSpec((1, d, tf), lambda e, m, f: (e, 0, f)),
                  pl.BlockSpec((1, tf, d), lambda e, m, f: (e, f, 0)),
                  pl.BlockSpec((nb, 1, cap, LANES), lambda e, m, f: (m, e, 0, 0))],
        out_specs=pl.BlockSpec((nb, 1, cap, d), lambda e, m, f: (m, e, 0, 0)),
        out_shape=jax.ShapeDtypeStruct((bsz, ne, cap, d), BF16),
        scratch_shapes=[pltpu.VMEM((nb, cap, d), F32)],
        compiler_params=_cparams(("arbitrary", "arbitrary", "arbitrary"), 56),
        name="ffn",
    )(xg, w_gate, w_up, w_down, wrep)


def _combine_kernel(offs_ref, grank_ref, eo_ref, h_ref, mod_ref, gf_ref, o_ref, y_ref, *,
                    ne, nt, tt, cap, nsub, final):
    b = pl.program_id(0)
    j = pl.program_id(1)
    plans = [_round_plan(offs_ref, b, j * nsub + sub, ne, nt, cap) for sub in range(nsub)]

    def round_sum(sub, r):
        cols = slice(sub * tt, (sub + 1) * tt)
        starts, masks = _one_hot_windows(grank_ref[0, :, cols], plans[sub][0], r, ne, tt, cap)
        pcat = jnp.concatenate([jnp.where(mk, 1.0, 0.0).astype(BF16) for mk in masks], axis=0)
        ocat = jnp.concatenate([eo_ref[0, e, pl.ds(starts[e], SLOT_WIN), :] for e in range(ne)], axis=0)
        return lax.dot_general(pcat, ocat, (((0,), (0,)), ((), ())), preferred_element_type=F32)

    for sub in range(nsub):
        y_ref[sub * tt:(sub + 1) * tt, :] = round_sum(sub, 0)
    for sub in range(nsub):
        def extra(r, carry, sub=sub):
            y_ref[sub * tt:(sub + 1) * tt, :] += round_sum(sub, r)
            return carry
        lax.fori_loop(1, plans[sub][1], extra, 0)
    hout = h_ref[0] + mod_ref[0, 5:6, :] * y_ref[...]
    if final:
        hout = hout * lax.rsqrt(jnp.mean(hout * hout, axis=-1, keepdims=True) + RMS_EPS) * gf_ref[...]
    o_ref[0] = hout


def _combine(offs, grank, eo, h, mod, gf, tt, nsub, final):
    bsz, s, d = h.shape
    ne, cap = eo.shape[1], eo.shape[2]
    nt = s // tt
    tg = nsub * tt
    grid_spec = pltpu.PrefetchScalarGridSpec(
        num_scalar_prefetch=1,
        grid=(bsz, nt // nsub),
        in_specs=[pl.BlockSpec((1, ne, tg), lambda b, j, o: (b, 0, j)),
                  pl.BlockSpec((1, ne, cap, d), lambda b, j, o: (b, 0, 0, 0)),
                  pl.BlockSpec((1, tg, d), lambda b, j, o: (b, j, 0)),
                  pl.BlockSpec((1, N_MOD, d), lambda b, j, o: (b, 0, 0)),
                  pl.BlockSpec((1, d), lambda b, j, o: (0, 0))],
        out_specs=pl.BlockSpec((1, tg, d), lambda b, j, o: (b, j, 0)),
        scratch_shapes=[pltpu.VMEM((tg, d), F32)],
    )
    return pl.pallas_call(
        functools.partial(_combine_kernel, ne=ne, nt=nt, tt=tt, cap=cap, nsub=nsub, final=final),
        grid_spec=grid_spec,
        out_shape=jax.ShapeDtypeStruct((bsz, s, d), F32),
        compiler_params=_cparams(("arbitrary", "arbitrary"), 56),
        name="combine",
    )(offs, grank, eo, h, mod, gf)


def kernel(x, c, ada_w, ada_b, norm1_g, w_in, b_in, conv_dw_w, conv_dw_b, conv_ln_g, conv_ln_b,
           conv_w_out, conv_b_out, fourier_w, fourier_b, w_out, b_out, norm2_g, router_w,
           expert_w_gate, expert_w_up, expert_w_down, final_norm_g):
    bsz, s, d = x.shape
    depth = ada_w.shape[0]
    cw = conv_dw_w.shape[2]
    fw = fourier_w.shape[1]
    ne = router_w.shape[2]
    cap = CAPACITY_FACTOR * s // ne
    tt = min(ROUTE_TILE, s)
    ts = min(512, s)
    assert s % DFT_RADIX == 0 and s % tt == 0 and cap % BF16_ROWS == 0 and cap >= SLOT_WIN and ne <= LANES
    wg_np, tab_np = _dft_constants(s, fw)
    dft_wg = jnp.asarray(wg_np).astype(BF16)
    dft_tab = jnp.asarray(tab_np).astype(BF16)
    row = lambda a: a.reshape(1, -1)

    h = x
    for l in range(depth):
        mod = _ada(c, ada_w[l], ada_b[l]).reshape(bsz, N_MOD, d)
        c1 = 2 * cw + fw
        v, f = _proj_cf(h, mod, row(norm1_g[l]), w_in[l][:, :c1].astype(BF16), row(b_in[l][:c1]), cw, fw,
                        min(2 * ts, s))
        fr = _fourier(f, dft_wg, dft_tab)
        rw = jnp.zeros((d, LANES), BF16).at[:, :ne].set(router_w[l].astype(BF16))
        h, u2, lg = _mixer(h, mod, row(norm1_g[l]), w_in[l][:, c1:].astype(BF16), row(b_in[l][c1:]), v,
                           conv_dw_w[l], row(conv_dw_b[l]), row(conv_ln_g[l]), row(conv_ln_b[l]),
                           conv_w_out[l].astype(BF16), row(conv_b_out[l]), fr,
                           fourier_w[l].astype(BF16), row(fourier_b[l]),
                           w_out[l].astype(BF16), row(b_out[l]), row(norm2_g[l]), rw, ts)
        grank, aff, offs = _route(lg, ne, cap, tt)
        offs_flat = offs[:, :, :s // tt + 1].reshape(-1)
        nsub = 2 if (s // tt) % 2 == 0 else 1
        xg, wrep = _gather(offs_flat, u2, grank, aff, ne, cap, tt, nsub)
        nb = 4 if bsz % 4 == 0 else 1
        eo = _ffn(xg, wrep, expert_w_gate[l], expert_w_up[l], expert_w_down[l], nb, min(512, expert_w_gate.shape[3] // 2))
        h = _combine(offs_flat, grank, eo, h, mod, row(final_norm_g), tt, nsub, final=(l == depth - 1))
    return h
```
